```python
import math
import functools
import jax, jax.numpy as jnp
from jax import lax
import numpy as np

D_MODEL = 2048
BATCH = 4
SEQ = 4096
DEPTH = 1
DEC_BATCH = 128
DEC_SEQ = 8
PAST_LEN = 16384
PAGE_SIZE = 128

RET_HEADS = 8
RET_DK = 128
RET_DV = 256
RET_CHUNK = 128
RET_QK_W = RET_HEADS * RET_DK
RET_V_W = RET_HEADS * RET_DV
MLA_HEADS = 16
Q_LORA = 768
KV_LORA = 512
QK_NOPE = 128
QK_ROPE = 64
V_HEAD = 128
MLA_SCALE = (QK_NOPE + QK_ROPE) ** -0.5
ATTN_QBLOCK = 128
ROPE_THETA = 10000.0
PEER_HEADS = 8
N_KEYS = 128
N_EXPERTS = N_KEYS * N_KEYS
PEER_KDIM = 128
PEER_HALF = PEER_KDIM // 2
PEER_TOPK = 16
PEER_BLOCK = 128
PLE_DIM = 256
EPS = 1e-6
IN_SIZES = (RET_QK_W, RET_QK_W, RET_V_W, RET_V_W, Q_LORA, KV_LORA, QK_ROPE, D_MODEL, D_MODEL)
IN_W = 2 * RET_QK_W + 2 * RET_V_W + Q_LORA + KV_LORA + QK_ROPE + 2 * D_MODEL

kernel_name = 'hybrid_retention_mla_peer_step'


def _rms(x):
    xf = x.astype(jnp.float32)
    return xf * lax.rsqrt(jnp.mean(xf * xf, axis=-1, keepdims=True) + EPS)


def rms_norm(x, g):
    return (_rms(x) * g.astype(jnp.float32)).astype(x.dtype)


def rope(x, pos):
    half = x.shape[-1] // 2
    inv = ROPE_THETA ** (-jnp.arange(half, dtype=jnp.float32) / half)
    ang = pos.astype(jnp.float32)[:, None] * inv[None, :]
    cos = jnp.cos(ang)[:, None, :]
    sin = jnp.sin(ang)[:, None, :]
    xf = x.astype(jnp.float32)
    x1, x2 = xf[..., :half], xf[..., half:]
    return jnp.concatenate([x1 * cos - x2 * sin, x2 * cos + x1 * sin], axis=-1).astype(x.dtype)


def retention(q, k, v, s0):
    B, L, H, DK = q.shape
    DV = v.shape[-1]
    C = math.gcd(L, RET_CHUNK)
    n = L // C
    lg = jnp.log(1.0 - 2.0 ** (-5.0 - jnp.arange(H, dtype=jnp.float32)))
    idx = jnp.arange(C, dtype=jnp.float32)
    diff = idx[:, None] - idx[None, :]
    causal = diff >= 0
    dmat = jnp.where(causal[None], jnp.exp(lg[:, None, None] * jnp.where(causal, diff, 0.0)[None]), 0.0)
    q_dec = jnp.exp(lg[:, None] * (idx[None, :] + 1.0))
    k_dec = jnp.exp(lg[:, None] * (C - 1.0 - idx[None, :]))
    s_dec = jnp.exp(lg * C)

    def chunks(t):
        return t.astype(jnp.float32).reshape(B, n, C, H, t.shape[-1]).transpose(1, 0, 3, 2, 4)

    def step(s, xs):
        qb, kb, vb = xs
        att = jnp.einsum('bhid,bhjd->bhij', qb, kb) * dmat[None]
        o = (jnp.einsum('bhij,bhjv->bhiv', att, vb)
             + jnp.einsum('bhid,bhdv->bhiv', qb * q_dec[None, :, :, None], s))
        s = s * s_dec[None, :, None, None] + jnp.einsum('bhjd,bhjv->bhdv', kb * k_dec[None, :, :, None], vb)
        return s, o

    s, o = lax.scan(step, s0.astype(jnp.float32), (chunks(q), chunks(k), chunks(v)))
    o = o.transpose(1, 0, 3, 2, 4).reshape(B, L, H, DV)
    return o, s


def mla_prompt_attend(q_nope, q_rope, c_kv, k_rope, w_ukv):
    B, L, H, _ = q_nope.shape
    kv = jnp.einsum('blc,che->blhe', c_kv, w_ukv)
    k_nope, v = kv[..., :QK_NOPE], kv[..., QK_NOPE:]
    QB = math.gcd(L, ATTN_QBLOCK)
    nb = L // QB

    def blocks(t):
        return t.reshape((B, nb, QB) + t.shape[2:]).swapaxes(0, 1)

    kpos = jnp.arange(L)

    def one_block(args):
        qn, qr, start = args
        s = (jnp.einsum('bqhd,bkhd->bhqk', qn, k_nope)
             + jnp.einsum('bqhr,bkr->bhqk', qr, k_rope)).astype(jnp.float32) * MLA_SCALE
        qpos = start + jnp.arange(QB)
        s = jnp.where(kpos[None, :] <= qpos[:, None], s, -jnp.inf)
        p = jax.nn.softmax(s, axis=-1)
        return jnp.einsum('bhqk,bkhd->bqhd', p.astype(v.dtype), v)

    o = lax.map(one_block, (blocks(q_nope), blocks(q_rope), jnp.arange(nb) * QB))
    return o.swapaxes(0, 1).reshape(B, L, H, V_HEAD)


def mla_sample_attend(q_nope, q_rope, c_kv, k_rope, w_ukv, cache_ckv, cache_krope, page_table):
    B, L, H, _ = q_nope.shape
    w_uk, w_uv = w_ukv[..., :QK_NOPE], w_ukv[..., QK_NOPE:]
    q_lat = jnp.einsum('blhd,chd->blhc', q_nope, w_uk)

    def score(ckv, kr):
        return (jnp.einsum('blhc,bkc->bhlk', q_lat, ckv)
                + jnp.einsum('blhr,bkr->bhlk', q_rope, kr)).astype(jnp.float32) * MLA_SCALE

    def absorb(carry, s, ckv):
        m, l, acc = carry
        m_new = jnp.maximum(m, s.max(axis=-1))
        corr = jnp.exp(m - m_new)
        p = jnp.exp(s - m_new[..., None])
        acc = acc * corr[..., None] + jnp.einsum('bhlk,bkc->bhlc', p, ckv.astype(jnp.float32))
        return (m_new, l * corr + p.sum(axis=-1), acc)

    def page_step(carry, pages):
        ckv = cache_ckv[pages]
        kr = cache_krope[pages]
        return absorb(carry, score(ckv, kr), ckv), None

    init = (jnp.full((B, H, L), -1e30, jnp.float32),
            jnp.zeros((B, H, L), jnp.float32),
            jnp.zeros((B, H, L, KV_LORA), jnp.float32))
    carry, _ = lax.scan(page_step, init, page_table.T)
    causal = jnp.tril(jnp.ones((L, L), dtype=bool))
    s_self = jnp.where(causal, score(c_kv, k_rope), -jnp.inf)
    _, l, acc = absorb(carry, s_self, c_kv)
    o_lat = acc / l[..., None]
    return jnp.einsum('bhlc,chd->blhd', o_lat.astype(w_uv.dtype), w_uv)


def peer(x, w_q, sub_keys, u_emb, v_emb):
    T, D = x.shape
    Tp = -(-T // PEER_BLOCK) * PEER_BLOCK
    xp = jnp.pad(x, ((0, Tp - T), (0, 0)))

    def blk(xb):
        q = (xb @ w_q).reshape(PEER_BLOCK, PEER_HEADS, 2, PEER_HALF)
        s = jnp.einsum('thpd,hpnd->thpn', q, sub_keys).astype(jnp.float32)
        sv, si = lax.top_k(s, PEER_TOPK)
        cand = (sv[:, :, 0, :, None] + sv[:, :, 1, None, :]).reshape(PEER_BLOCK, PEER_HEADS, PEER_TOPK * PEER_TOPK)
        cidx = (si[:, :, 0, :, None] * N_KEYS + si[:, :, 1, None, :]).reshape(PEER_BLOCK, PEER_HEADS, PEER_TOPK * PEER_TOPK)
        fv, fpos = lax.top_k(cand, PEER_TOPK)
        eidx = jnp.take_along_axis(cidx, fpos, axis=-1)
        g = jax.nn.softmax(fv, axis=-1)
        a = jnp.einsum('td,thkd->thk', xb, u_emb[eidx]).astype(jnp.float32)
        w = (g * jax.nn.gelu(a, approximate=False)).astype(xb.dtype)
        return jnp.einsum('thk,thkd->td', w, v_emb[eidx])

    y = lax.map(blk, xp.reshape(Tp // PEER_BLOCK, PEER_BLOCK, D)).reshape(Tp, D)
    return y[:T]


def hybrid_layer(x, p_emb, pos, ret_s0, attend, norm1_g, w_in, q_norm_g, w_uq, kv_norm_g, w_ukv,
                 w_ret_o, w_mla_o, w_o, norm2_g, peer_wq, peer_keys, peer_u, peer_v,
                 norm3_g, w_ple_gate, w_ple):
    B, L, D = x.shape
    n = rms_norm(x, norm1_g)
    cuts = np.cumsum(IN_SIZES)[:-1].tolist()
    rq, rk, rv, rg, cq, ckv, kr, ga, gb = jnp.split(n @ w_in, cuts, axis=-1)
    rq = rope(rq.reshape(B, L, RET_HEADS, RET_DK), pos)
    rk = rope(rk.reshape(B, L, RET_HEADS, RET_DK), pos) * (RET_DK ** -0.5)
    rv = rv.reshape(B, L, RET_HEADS, RET_DV)
    ro, ret_s = retention(rq, rk, rv, ret_s0)
    ro = _rms(ro).reshape(B, L, RET_V_W).astype(x.dtype) * jax.nn.silu(rg)
    ro = ro @ w_ret_o
    cq = rms_norm(cq, q_norm_g)
    q = jnp.einsum('blc,che->blhe', cq, w_uq)
    q_nope, q_rope = q[..., :QK_NOPE], rope(q[..., QK_NOPE:], pos)
    ckv = rms_norm(ckv, kv_norm_g)
    kr = rope(kr[:, :, None, :], pos)[:, :, 0, :]
    mo = attend(q_nope, q_rope, ckv, kr, w_ukv).reshape(B, L, MLA_HEADS * V_HEAD) @ w_mla_o
    merged = jax.nn.sigmoid(ga) * ro + jax.nn.sigmoid(gb) * mo
    h = x + merged @ w_o
    h = h + peer(rms_norm(h, norm2_g).reshape(B * L, D), peer_wq, peer_keys, peer_u, peer_v).reshape(B, L, D)
    gate = jax.nn.sigmoid(rms_norm(h, norm3_g) @ w_ple_gate)
    h = h + gate * (p_emb @ w_ple)
    return h, ckv, kr, ret_s


def setup_inputs(seed: int = 0) -> dict:
    key = jax.random.key(seed)
    ks = jax.random.split(key, 32)
    f32 = jnp.float32
    n_pages = PAST_LEN // PAGE_SIZE
    n_used = DEC_BATCH * n_pages
    n_pool = n_used + n_used // 4

    def nrm(k, shape, scale):
        return jax.random.normal(k, shape, f32) * scale

    def gain(k, shape):
        return 1.0 + 0.05 * jax.random.normal(k, shape, f32)

    perm = jax.random.permutation(ks[5], n_pool).astype(jnp.int32)
    page_table = perm[:n_used].reshape(DEC_BATCH, n_pages)
    return {
        'x_prompt': nrm(ks[0], (BATCH, SEQ, D_MODEL), 1.0),
        'x_sample': nrm(ks[1], (DEC_BATCH, DEC_SEQ, D_MODEL), 1.0),
        'cache_ckv': nrm(ks[2], (DEPTH, n_pool, PAGE_SIZE, KV_LORA), 1.0),
        'cache_krope': nrm(ks[3], (DEPTH, n_pool, PAGE_SIZE, QK_ROPE), 1.0),
        'state_ret': nrm(ks[4], (DEPTH, DEC_BATCH, RET_HEADS, RET_DK, RET_DV), 0.5),
        'page_table': page_table,
        'p_prompt': nrm(ks[6], (DEPTH, BATCH, SEQ, PLE_DIM), 1.0),
        'p_sample': nrm(ks[7], (DEPTH, DEC_BATCH, DEC_SEQ, PLE_DIM), 1.0),
        'norm1_g': gain(ks[8], (DEPTH, D_MODEL)),
        'w_in': nrm(ks[9], (DEPTH, D_MODEL, IN_W), D_MODEL ** -0.5),
        'q_norm_g': gain(ks[10], (DEPTH, Q_LORA)),
        'w_uq': nrm(ks[11], (DEPTH, Q_LORA, MLA_HEADS, QK_NOPE + QK_ROPE), Q_LORA ** -0.5),
        'kv_norm_g': gain(ks[12], (DEPTH, KV_LORA)),
        'w_ukv': nrm(ks[13], (DEPTH, KV_LORA, MLA_HEADS, QK_NOPE + V_HEAD), KV_LORA ** -0.5),
        'w_ret_o': nrm(ks[14], (DEPTH, RET_V_W, D_MODEL), RET_V_W ** -0.5),
        'w_mla_o': nrm(ks[15], (DEPTH, MLA_HEADS * V_HEAD, D_MODEL), (MLA_HEADS * V_HEAD) ** -0.5),
        'w_o': nrm(ks[16], (DEPTH, D_MODEL, D_MODEL), D_MODEL ** -0.5),
        'norm2_g': gain(ks[17], (DEPTH, D_MODEL)),
        'peer_wq': nrm(ks[18], (DEPTH, D_MODEL, PEER_HEADS * PEER_KDIM), D_MODEL ** -0.5),
        'peer_keys': nrm(ks[19], (DEPTH, PEER_HEADS, 2, N_KEYS, PEER_HALF), PEER_HALF ** -0.5),
        'peer_u': nrm(ks[20], (DEPTH, N_EXPERTS, D_MODEL), D_MODEL ** -0.5),
        'peer_v': nrm(ks[21], (DEPTH, N_EXPERTS, D_MODEL), PEER_HEADS ** -0.5),
        'norm3_g': gain(ks[22], (DEPTH, D_MODEL)),
        'w_ple_gate': nrm(ks[23], (DEPTH, D_MODEL, D_MODEL), D_MODEL ** -0.5),
        'w_ple': nrm(ks[24], (DEPTH, PLE_DIM, D_MODEL), PLE_DIM ** -0.5),
        'final_norm_g': gain(ks[25], (D_MODEL,)),
    }


def reference(x_prompt, x_sample, cache_ckv, cache_krope, state_ret, page_table, p_prompt, p_sample,
              norm1_g, w_in, q_norm_g, w_uq, kv_norm_g, w_ukv, w_ret_o, w_mla_o, w_o, norm2_g,
              peer_wq, peer_keys, peer_u, peer_v, norm3_g, w_ple_gate, w_ple, final_norm_g):
    past_len = page_table.shape[1] * PAGE_SIZE
    pos_p = jnp.arange(x_prompt.shape[1], dtype=jnp.int32)
    pos_s = past_len + jnp.arange(x_sample.shape[1], dtype=jnp.int32)
    ret0_p = jnp.zeros((x_prompt.shape[0], RET_HEADS, RET_DK, RET_DV), jnp.float32)
    hp, hs = x_prompt, x_sample
    ckv_p, kr_p, rs_p, ckv_s, kr_s, rs_s = [], [], [], [], [], []
    for i in range(DEPTH):
        lw = (norm1_g[i], w_in[i], q_norm_g[i], w_uq[i], kv_norm_g[i], w_ukv[i], w_ret_o[i], w_mla_o[i],
              w_o[i], norm2_g[i], peer_wq[i], peer_keys[i], peer_u[i], peer_v[i], norm3_g[i],
              w_ple_gate[i], w_ple[i])
        hp, c1, k1, s1 = hybrid_layer(hp, p_prompt[i], pos_p, ret0_p, mla_prompt_attend, *lw)
        attend_s = functools.partial(mla_sample_attend, cache_ckv=cache_ckv[i],
                                     cache_krope=cache_krope[i], page_table=page_table)
        hs, c2, k2, s2 = hybrid_layer(hs, p_sample[i], pos_s, state_ret[i], attend_s, *lw)
        ckv_p.append(c1); kr_p.append(k1); rs_p.append(s1)
        ckv_s.append(c2); kr_s.append(k2); rs_s.append(s2)
    y_prompt = rms_norm(hp, final_norm_g)
    y_sample = rms_norm(hs, final_norm_g)
    new_ckv_prompt = jnp.stack(ckv_p)
    new_krope_prompt = jnp.stack(kr_p)
    new_state_ret_prompt = jnp.stack(rs_p)
    new_ckv_sample = jnp.stack(ckv_s)
    new_krope_sample = jnp.stack(kr_s)
    new_state_ret_sample = jnp.stack(rs_s)
    return (y_prompt, y_sample, new_ckv_prompt, new_krope_prompt, new_state_ret_prompt, new_ckv_sample, new_krope_sample, new_state_ret_sample)
```

```python
import functools
import math

import jax
import jax.numpy as jnp
from jax import lax
from jax.experimental import pallas as pl
from jax.experimental.pallas import tpu as pltpu

F32 = jnp.float32
BF16 = jnp.bfloat16

D_MODEL = 2048
RET_HEADS = 8
RET_DK = 128
RET_DV = 256
RET_CHUNK = 128
RET_QK_W = RET_HEADS * RET_DK
RET_V_W = RET_HEADS * RET_DV
MLA_HEADS = 16
Q_LORA = 768
KV_LORA = 512
QK_NOPE = 128
QK_ROPE = 64
V_HEAD = 128
MLA_SCALE = (QK_NOPE + QK_ROPE) ** -0.5
ROPE_THETA = 10000.0
PEER_HEADS = 8
N_KEYS = 128
PEER_KDIM = 128
PEER_HALF = PEER_KDIM // 2
PEER_TOPK = 16
PLE_DIM = 256
EPS = 1e-6
PAGE_SIZE = 128

LANES = 128
QK_PAD = 2 * LANES
SMALL_W = Q_LORA + KV_LORA + LANES
BIG_W = 2 * RET_QK_W + 2 * RET_V_W + 2 * D_MODEL
VMEM_LIMIT = 56 * 1024 * 1024
NEG_BIG = -1e30

_NT = (((1,), (1,)), ((), ()))
_TN = (((0,), (0,)), ((), ()))


def _params(*sem):
    return pltpu.CompilerParams(dimension_semantics=sem, vmem_limit_bytes=VMEM_LIMIT)


def _rms_scale(x):
    return lax.rsqrt(jnp.mean(x * x, axis=-1, keepdims=True) + EPS)


def _half_swap(x):
    return pltpu.roll(x, LANES // 2, 1)


def _norm_matmul_kernel(x_ref, g_ref, w_ref, o_ref, xn_ref):
    @pl.when(pl.program_id(1) == 0)
    def _():
        x = x_ref[...]
        xn_ref[...] = (x * _rms_scale(x) * g_ref[...]).astype(BF16)

    o_ref[...] = jnp.dot(xn_ref[...], w_ref[...], preferred_element_type=F32).astype(o_ref.dtype)


def _norm_matmul(x, g, w, out_dtype, tm, tn):
    t, d = x.shape
    n = w.shape[1]
    tm = min(tm, t)
    return pl.pallas_call(
        _norm_matmul_kernel,
        grid=(t // tm, n // tn),
        in_specs=[
            pl.BlockSpec((tm, d), lambda i, j: (i, 0)),
            pl.BlockSpec((1, d), lambda i, j: (0, 0)),
            pl.BlockSpec((d, tn), lambda i, j: (0, j)),
        ],
        out_specs=pl.BlockSpec((tm, tn), lambda i, j: (i, j)),
        out_shape=jax.ShapeDtypeStruct((t, n), out_dtype),
        scratch_shapes=[pltpu.VMEM((tm, d), BF16)],
        compiler_params=_params("parallel", "arbitrary"),
        name="norm_matmul",
    )(x, g.reshape(1, d), w)


def _retention_consts(c):
    lg = jnp.log(1.0 - 2.0 ** (-5.0 - jnp.arange(RET_HEADS, dtype=F32)))
    idx = jnp.arange(c, dtype=F32)
    diff = idx[:, None] - idx[None, :]
    causal = diff >= 0
    dmat = jnp.where(causal[None], jnp.exp(lg[:, None, None] * jnp.where(causal, diff, 0.0)[None]), 0.0)
    q_dec = jnp.exp(lg[:, None] * (idx[None, :] + 1.0))
    k_dec = jnp.exp(lg[:, None] * (c - 1.0 - idx[None, :]))
    s_dec = jnp.exp(lg * c)
    qd = jnp.broadcast_to(q_dec[:, :, None], (RET_HEADS, c, RET_DK))
    kd = jnp.broadcast_to(k_dec[:, :, None], (RET_HEADS, c, RET_DK))
    sd = jnp.broadcast_to(s_dec[:, None, None], (RET_HEADS, 8, RET_DV))
    return dmat, qd, kd, sd


def _rope_tables(pos, half):
    inv = ROPE_THETA ** (-jnp.arange(half, dtype=F32) / half)
    ang = pos.astype(F32)[:, None] * inv[None, :]
    return jnp.cos(ang), jnp.sin(ang)


def _ret_head(q, k, v, s, dmat, qd, kd, sd):
    qb = q.astype(BF16)
    kb = k.astype(BF16)
    att = lax.dot_general(qb, kb, _NT, preferred_element_type=F32) * dmat
    o = jnp.dot(att.astype(BF16), v, preferred_element_type=F32)
    o = o + jnp.dot((q * qd).astype(BF16), s.astype(BF16), preferred_element_type=F32)
    s_new = s * sd + lax.dot_general((k * kd).astype(BF16), v, _TN, preferred_element_type=F32)
    return o, s_new


def _ret_prompt_kernel(rq_ref, rk_ref, rv_ref, rg_ref, cos_ref, sin_ref, dmat_ref, qd_ref, kd_ref, sd_ref,
                       o_ref, s_ref, *, n_sub):
    @pl.when(pl.program_id(1) == 0)
    def _():
        s_ref[...] = jnp.zeros_like(s_ref)

    c = RET_CHUNK
    for sub in range(n_sub):
        rows = slice(sub * c, (sub + 1) * c)
        cos = cos_ref[rows, :]
        sin = sin_ref[rows, :]
        for h in range(RET_HEADS):
            qk = slice(h * RET_DK, (h + 1) * RET_DK)
            vv = slice(h * RET_DV, (h + 1) * RET_DV)
            q = rq_ref[rows, qk].astype(F32)
            k = rk_ref[rows, qk].astype(F32)
            q = q * cos + _half_swap(q) * sin
            k = (k * cos + _half_swap(k) * sin) * (RET_DK ** -0.5)
            o, s_new = _ret_head(q, k, rv_ref[rows, vv], s_ref[0, h], dmat_ref[h], qd_ref[h], kd_ref[h],
                                 sd_ref[h, 0:1, :])
            s_ref[0, h] = s_new
            rg = rg_ref[rows, vv].astype(F32)
            o_ref[rows, vv] = (o * _rms_scale(o) * (rg * jax.nn.sigmoid(rg))).astype(BF16)


def _retention_prompt(proj, b, l, rb):
    rb = min(rb, l)
    n_r = l // rb
    cos, sin = _rope_tables(jnp.arange(l, dtype=jnp.int32), RET_DK // 2)
    cos2 = jnp.concatenate([cos, cos], axis=1)
    sin2 = jnp.concatenate([-sin, sin], axis=1)
    dmat, qd, kd, sd = _retention_consts(RET_CHUNK)
    row = lambda bi, r: bi * n_r + r
    const3 = lambda bi, r: (0, 0, 0)
    return pl.pallas_call(
        functools.partial(_ret_prompt_kernel, n_sub=rb // RET_CHUNK),
        grid=(b, n_r),
        in_specs=[
            pl.BlockSpec((rb, RET_QK_W), lambda bi, r: (row(bi, r), 0)),
            pl.BlockSpec((rb, RET_QK_W), lambda bi, r: (row(bi, r), 1)),
            pl.BlockSpec((rb, RET_V_W), lambda bi, r: (row(bi, r), 1)),
            pl.BlockSpec((rb, RET_V_W), lambda bi, r: (row(bi, r), 2)),
            pl.BlockSpec((rb, RET_DK), lambda bi, r: (r, 0)),
            pl.BlockSpec((rb, RET_DK), lambda bi, r: (r, 0)),
            pl.BlockSpec(dmat.shape, const3),
            pl.BlockSpec(qd.shape, const3),
            pl.BlockSpec(kd.shape, const3),
            pl.BlockSpec(sd.shape, const3),
        ],
        out_specs=[
            pl.BlockSpec((rb, RET_V_W), lambda bi, r: (row(bi, r), 0)),
            pl.BlockSpec((1, RET_HEADS, RET_DK, RET_DV), lambda bi, r: (bi, 0, 0, 0)),
        ],
        out_shape=[
            jax.ShapeDtypeStruct((b * l, RET_V_W), BF16),
            jax.ShapeDtypeStruct((b, RET_HEADS, RET_DK, RET_DV), F32),
        ],
        compiler_params=_params("parallel", "arbitrary"),
        name="retention_prompt",
    )(proj, proj, proj, proj, cos2, sin2, dmat, qd, kd, sd)


def _ret_sample_kernel(rq_ref, rk_ref, rv_ref, rg_ref, cos_ref, sin_ref, dmat_ref, qd_ref, kd_ref, sd_ref, s0_ref,
                       o_ref, s_ref, *, l):
    cos = cos_ref[...]
    sin = sin_ref[...]
    first = lax.broadcasted_iota(jnp.int32, (2 * l, 1), 0) < l
    for h in range(RET_HEADS):
        qk = slice(h * RET_DK, (h + 1) * RET_DK)
        vv = slice(h * RET_DV, (h + 1) * RET_DV)
        q = rq_ref[:, qk].astype(F32)
        k = rk_ref[:, qk].astype(F32)
        q = q * cos + _half_swap(q) * sin
        k = (k * cos + _half_swap(k) * sin) * (RET_DK ** -0.5)
        v = rv_ref[:, vv]
        att = lax.dot_general(q.astype(BF16), k.astype(BF16), _NT, preferred_element_type=F32) * dmat_ref[h]
        o = jnp.dot(att.astype(BF16), v, preferred_element_type=F32)
        qdb = (q * qd_ref[h]).astype(BF16)
        kdk = k * kd_ref[h]
        sd = sd_ref[h, 0:1, :]
        s_a = s0_ref[0, h]
        s_b = s0_ref[1, h]
        o = o + jnp.where(first,
                          jnp.dot(qdb, s_a.astype(BF16), preferred_element_type=F32),
                          jnp.dot(qdb, s_b.astype(BF16), preferred_element_type=F32))
        k_a = jnp.where(first, kdk, 0.0).astype(BF16)
        k_b = jnp.where(first, 0.0, kdk).astype(BF16)
        s_ref[0, h] = s_a * sd + lax.dot_general(k_a, v, _TN, preferred_element_type=F32)
        s_ref[1, h] = s_b * sd + lax.dot_general(k_b, v, _TN, preferred_element_type=F32)
        rg = rg_ref[:, vv].astype(F32)
        o_ref[:, vv] = (o * _rms_scale(o) * (rg * jax.nn.sigmoid(rg))).astype(BF16)


def _retention_sample(proj, state, b, l, pos0):
    assert b % 2 == 0
    cos, sin = _rope_tables(pos0 + jnp.arange(l, dtype=jnp.int32), RET_DK // 2)
    cos2 = jnp.tile(jnp.concatenate([cos, cos], axis=1), (2, 1))
    sin2 = jnp.tile(jnp.concatenate([-sin, sin], axis=1), (2, 1))
    dmat, qd, kd, sd = _retention_consts(l)
    zeros = jnp.zeros_like(dmat)
    dmat2 = jnp.concatenate([jnp.concatenate([dmat, zeros], axis=2), jnp.concatenate([zeros, dmat], axis=2)], axis=1)
    qd2 = jnp.tile(qd, (1, 2, 1))
    kd2 = jnp.tile(kd, (1, 2, 1))
    rows = 2 * l
    const2 = lambda i: (0, 0)
    const3 = lambda i: (0, 0, 0)
    return pl.pallas_call(
        functools.partial(_ret_sample_kernel, l=l),
        grid=(b // 2,),
        in_specs=[
            pl.BlockSpec((rows, RET_QK_W), lambda i: (i, 0)),
            pl.BlockSpec((rows, RET_QK_W), lambda i: (i, 1)),
            pl.BlockSpec((rows, RET_V_W), lambda i: (i, 1)),
            pl.BlockSpec((rows, RET_V_W), lambda i: (i, 2)),
            pl.BlockSpec((rows, RET_DK), const2),
            pl.BlockSpec((rows, RET_DK), const2),
            pl.BlockSpec(dmat2.shape, const3),
            pl.BlockSpec(qd2.shape, const3),
            pl.BlockSpec(kd2.shape, const3),
            pl.BlockSpec(sd.shape, const3),
            pl.BlockSpec((2, RET_HEADS, RET_DK, RET_DV), lambda i: (i, 0, 0, 0)),
        ],
        out_specs=[
            pl.BlockSpec((rows, RET_V_W), lambda i: (i, 0)),
            pl.BlockSpec((2, RET_HEADS, RET_DK, RET_DV), lambda i: (i, 0, 0, 0)),
        ],
        out_shape=[
            jax.ShapeDtypeStruct((b * l, RET_V_W), BF16),
            jax.ShapeDtypeStruct((b, RET_HEADS, RET_DK, RET_DV), F32),
        ],
        compiler_params=_params("parallel"),
        name="retention_sample",
    )(proj, proj, proj, proj, cos2, sin2, dmat2, qd2, kd2, sd, state)


def _mla_proj_kernel(x_ref, qg_ref, kvg_ref, wq_ref, wk_ref, wv_ref, cos_ref, sin_ref,
                     q_ref, ckv_ref, kr_ref, *kv_refs):
    cos = cos_ref[...]
    sin = sin_ref[...]
    cq = x_ref[:, 0:Q_LORA]
    cqn = (cq * _rms_scale(cq) * qg_ref[...]).astype(BF16)
    q = jnp.dot(cqn, wq_ref[...], preferred_element_type=F32)
    for h in range(MLA_HEADS):
        lo = h * QK_PAD
        q_ref[:, lo:lo + LANES] = q[:, lo:lo + LANES].astype(BF16)
        r = q[:, lo + LANES:lo + QK_PAD]
        q_ref[:, lo + LANES:lo + QK_PAD] = (r * cos + _half_swap(r) * sin).astype(BF16)
    ckv = x_ref[:, Q_LORA:Q_LORA + KV_LORA]
    ckvn = ckv * _rms_scale(ckv) * kvg_ref[...]
    ckv_ref[...] = ckvn
    krx = x_ref[:, Q_LORA + KV_LORA:SMALL_W]
    krr = krx * cos + _half_swap(krx) * sin
    kr_ref[...] = krr[:, 0:QK_ROPE]
    if kv_refs:
        k_ref, v_ref = kv_refs
        cb = ckvn.astype(BF16)
        kn = jnp.dot(cb, wk_ref[...], preferred_element_type=F32)
        krb = krr.astype(BF16)
        for h in range(MLA_HEADS):
            lo = h * QK_PAD
            k_ref[:, lo:lo + LANES] = kn[:, h * QK_NOPE:(h + 1) * QK_NOPE].astype(BF16)
            k_ref[:, lo + LANES:lo + QK_PAD] = krb
        v_ref[...] = jnp.dot(cb, wv_ref[...], preferred_element_type=F32).astype(BF16)


def _mla_proj(small, q_norm_g, kv_norm_g, wq, wk, wv, cosq, sinq, tm, with_kv):
    t = small.shape[0]
    tm = min(tm, t)
    rowb = lambda n: pl.BlockSpec((tm, n), lambda i: (i, 0))
    full = lambda a: pl.BlockSpec(a.shape, lambda i: (0, 0))
    qg = q_norm_g.reshape(1, Q_LORA)
    kvg = kv_norm_g.reshape(1, KV_LORA)
    out_specs = [rowb(MLA_HEADS * QK_PAD), rowb(KV_LORA), rowb(QK_ROPE)]
    out_shape = [
        jax.ShapeDtypeStruct((t, MLA_HEADS * QK_PAD), BF16),
        jax.ShapeDtypeStruct((t, KV_LORA), F32),
        jax.ShapeDtypeStruct((t, QK_ROPE), F32),
    ]
    if with_kv:
        out_specs += [rowb(MLA_HEADS * QK_PAD), rowb(MLA_HEADS * V_HEAD)]
        out_shape += [
            jax.ShapeDtypeStruct((t, MLA_HEADS * QK_PAD), BF16),
            jax.ShapeDtypeStruct((t, MLA_HEADS * V_HEAD), BF16),
        ]
    return pl.pallas_call(
        _mla_proj_kernel,
        grid=(t // tm,),
        in_specs=[rowb(SMALL_W), full(qg), full(kvg), full(wq), full(wk), full(wv), rowb(LANES), rowb(LANES)],
        out_specs=out_specs,
        out_shape=out_shape,
        compiler_params=_params("parallel"),
        name="mla_proj",
    )(small, qg, kvg, wq, wk, wv, cosq, sinq)


def _flash_kernel(q_ref, k_ref, v_ref, o_ref, *, tq):
    qi = pl.program_id(2)
    q = q_ref[...]

    def scores(j):
        off = pl.multiple_of(j * tq, tq)
        k = k_ref[pl.ds(off, tq), :]
        v = v_ref[pl.ds(off, tq), :]
        s = lax.dot_general(q, k, _NT, preferred_element_type=F32) * MLA_SCALE
        return s, v

    def update(carry, s, v):
        m, l, acc = carry
        m_new = jnp.maximum(m, jnp.max(s, axis=-1, keepdims=True))
        corr = jnp.exp(m - m_new)
        p = jnp.exp(s - m_new)
        l = l * corr + jnp.sum(p, axis=-1, keepdims=True)
        acc = acc * corr + jnp.dot(p.astype(BF16), v, preferred_element_type=F32)
        return m_new, l, acc

    def body(j, carry):
        s, v = scores(j)
        return update(carry, s, v)

    init = (jnp.full((tq, 1), NEG_BIG, F32), jnp.zeros((tq, 1), F32), jnp.zeros((tq, V_HEAD), F32))
    carry = lax.fori_loop(0, qi, body, init)
    s, v = scores(qi)
    row = lax.broadcasted_iota(jnp.int32, (tq, tq), 0)
    col = lax.broadcasted_iota(jnp.int32, (tq, tq), 1)
    _, l, acc = update(carry, jnp.where(col <= row, s, -jnp.inf), v)
    o_ref[...] = (acc / l).astype(BF16)


def _flash_attention(q, k, v, b, l, tq):
    tq = min(tq, l)
    n_q = l // tq
    return pl.pallas_call(
        functools.partial(_flash_kernel, tq=tq),
        grid=(b, MLA_HEADS, n_q),
        in_specs=[
            pl.BlockSpec((tq, QK_PAD), lambda bi, h, i: (bi * n_q + i, h)),
            pl.BlockSpec((l, QK_PAD), lambda bi, h, i: (bi, h)),
            pl.BlockSpec((l, V_HEAD), lambda bi, h, i: (bi, h)),
        ],
        out_specs=pl.BlockSpec((tq, V_HEAD), lambda bi, h, i: (bi * n_q + i, h)),
        out_shape=jax.ShapeDtypeStruct((b * l, MLA_HEADS * V_HEAD), BF16),
        compiler_params=_params("parallel", "parallel", "arbitrary"),
        name="mla_prompt_attention",
    )(q, k, v)


def _head_matmul_kernel(x_ref, w_ref, o_ref):
    o_ref[...] = jnp.dot(x_ref[...], w_ref[0], preferred_element_type=F32).astype(o_ref.dtype)


def _head_matmul(x, w, col_stride, out_dtype):
    m = x.shape[0]
    heads, kd, n = w.shape
    return pl.pallas_call(
        _head_matmul_kernel,
        grid=(heads,),
        in_specs=[
            pl.BlockSpec((m, kd), lambda h: (0, h * col_stride)),
            pl.BlockSpec((1, kd, n), lambda h: (h, 0, 0)),
        ],
        out_specs=pl.BlockSpec((m, n), lambda h: (0, h)),
        out_shape=jax.ShapeDtypeStruct((m, heads * n), out_dtype),
        compiler_params=_params("parallel"),
        name="head_matmul",
    )(x, w)


def _decode_kernel(pt_ref, ql_ref, qr_ref, *refs, pg, l_new):
    ckv_refs = refs[:pg]
    kr_refs = refs[pg:2 * pg]
    cn_ref, kn_ref, o_ref, m_ref, l_ref, acc_ref = refs[2 * pg:]
    g = pl.program_id(1)
    rows = ql_ref.shape[1]

    @pl.when(g == 0)
    def _():
        m_ref[...] = jnp.full_like(m_ref, NEG_BIG)
        l_ref[...] = jnp.zeros_like(l_ref)
        acc_ref[...] = jnp.zeros_like(acc_ref)

    ql = ql_ref[0]
    qr = qr_ref[0]

    def absorb(s, vals):
        m = m_ref[...]
        m_new = jnp.maximum(m, jnp.max(s, axis=-1, keepdims=True))
        corr = jnp.exp(m - m_new)
        p = jnp.exp(s - m_new[:, 0:1])
        l_ref[...] = l_ref[...] * corr + jnp.sum(p, axis=-1, keepdims=True)
        pv = acc_ref[...] * corr[:, 0:1]
        for i, val in enumerate(vals):
            w = val.shape[0]
            pv = pv + jnp.dot(p[:, i * w:(i + 1) * w].astype(BF16), val, preferred_element_type=F32)
        acc_ref[...] = pv
        m_ref[...] = m_new

    def score(ckv, kr):
        return (lax.dot_general(ql, ckv, _NT, preferred_element_type=F32)
                + lax.dot_general(qr, kr, _NT, preferred_element_type=F32)) * MLA_SCALE

    pages = [ckv_refs[i][0, 0].astype(BF16) for i in range(pg)]
    s = jnp.concatenate([score(pages[i], kr_refs[i][0, 0].astype(BF16)) for i in range(pg)], axis=1)
    absorb(s, pages)

    @pl.when(g == pl.num_programs(1) - 1)
    def _():
        cn = cn_ref[0].astype(BF16)
        s_new = score(cn, kn_ref[0].astype(BF16))
        tok = lax.broadcasted_iota(jnp.int32, s_new.shape, 0) // MLA_HEADS
        col = lax.broadcasted_iota(jnp.int32, s_new.shape, 1)
        absorb(jnp.where((col <= tok) & (col < l_new), s_new, -jnp.inf), [cn])
        o_ref[0] = (acc_ref[...] / l_ref[:, 0:1]).astype(BF16)


def _page_index(b, g, pt, *, i, pg):
    return (0, pt[b, g * pg + i], 0, 0)


def _decode_attention(q_lat, q_rope, cache_ckv, cache_krope, page_table, ckv_new, kr_new, pg):
    b, rows, _ = q_lat.shape
    n_pages = page_table.shape[1]
    l_new = ckv_new.shape[1]
    pad = 16 - l_new
    cn = jnp.pad(ckv_new, ((0, 0), (0, pad), (0, 0)))
    kn = jnp.pad(kr_new, ((0, 0), (0, pad), (0, 0)))
    per_b = lambda n: pl.BlockSpec((1, n[0], n[1]), lambda bi, g, pt: (bi, 0, 0))
    page_specs = (
        [pl.BlockSpec((1, 1, PAGE_SIZE, KV_LORA), functools.partial(_page_index, i=i, pg=pg)) for i in range(pg)]
        + [pl.BlockSpec((1, 1, PAGE_SIZE, QK_ROPE), functools.partial(_page_index, i=i, pg=pg)) for i in range(pg)])
    grid_spec = pltpu.PrefetchScalarGridSpec(
        num_scalar_prefetch=1,
        grid=(b, n_pages // pg),
        in_specs=[per_b((rows, KV_LORA)), per_b((rows, QK_ROPE))] + page_specs
        + [per_b((16, KV_LORA)), per_b((16, QK_ROPE))],
        out_specs=per_b((rows, KV_LORA)),
        scratch_shapes=[
            pltpu.VMEM((rows, LANES), F32),
            pltpu.VMEM((rows, LANES), F32),
            pltpu.VMEM((rows, KV_LORA), F32),
        ],
    )
    return pl.pallas_call(
        functools.partial(_decode_kernel, pg=pg, l_new=l_new),
        grid_spec=grid_spec,
        out_shape=jax.ShapeDtypeStruct((b, rows, KV_LORA), BF16),
        compiler_params=_params("parallel", "arbitrary"),
        name="mla_decode_attention",
    )(page_table, q_lat, q_rope, *([cache_ckv] * pg), *([cache_krope] * pg), cn, kn)


def _merge_kernel(ro_ref, mo_ref, wr_ref, wm_ref, ga_ref, gb_ref, o_ref):
    t1 = jnp.dot(ro_ref[...], wr_ref[...], preferred_element_type=F32)
    t2 = jnp.dot(mo_ref[...], wm_ref[...], preferred_element_type=F32)
    ga = jax.nn.sigmoid(ga_ref[...].astype(F32))
    gb = jax.nn.sigmoid(gb_ref[...].astype(F32))
    o_ref[...] = (ga * t1 + gb * t2).astype(BF16)


def _merge(ro, mo, w_ret_o, w_mla_o, proj, tm, tn):
    t = ro.shape[0]
    tm = min(tm, t)
    ga0 = (2 * RET_QK_W + 2 * RET_V_W) // tn
    gb0 = ga0 + D_MODEL // tn
    return pl.pallas_call(
        _merge_kernel,
        grid=(t // tm, D_MODEL // tn),
        in_specs=[
            pl.BlockSpec((tm, RET_V_W), lambda i, j: (i, 0)),
            pl.BlockSpec((tm, MLA_HEADS * V_HEAD), lambda i, j: (i, 0)),
            pl.BlockSpec((RET_V_W, tn), lambda i, j: (0, j)),
            pl.BlockSpec((MLA_HEADS * V_HEAD, tn), lambda i, j: (0, j)),
            pl.BlockSpec((tm, tn), lambda i, j: (i, ga0 + j)),
            pl.BlockSpec((tm, tn), lambda i, j: (i, gb0 + j)),
        ],
        out_specs=pl.BlockSpec((tm, tn), lambda i, j: (i, j)),
        out_shape=jax.ShapeDtypeStruct((t, D_MODEL), BF16),
        compiler_params=_params("parallel", "arbitrary"),
        name="branch_merge",
    )(ro, mo, w_ret_o, w_mla_o, proj, proj)


def _oproj_kernel(x_ref, m_ref, w_ref, g_ref, h_ref, xn_ref):
    h = x_ref[...] + jnp.dot(m_ref[...], w_ref[...], preferred_element_type=F32)
    h_ref[...] = h
    xn_ref[...] = (h * _rms_scale(h) * g_ref[...]).astype(BF16)


def _oproj(x, merged, w_o, norm2_g, tm):
    t = x.shape[0]
    tm = min(tm, t)
    rowb = pl.BlockSpec((tm, D_MODEL), lambda i: (i, 0))
    return pl.pallas_call(
        _oproj_kernel,
        grid=(t // tm,),
        in_specs=[rowb, rowb, pl.BlockSpec((D_MODEL, D_MODEL), lambda i: (0, 0)),
                  pl.BlockSpec((1, D_MODEL), lambda i: (0, 0))],
        out_specs=[rowb, rowb],
        out_shape=[jax.ShapeDtypeStruct((t, D_MODEL), F32), jax.ShapeDtypeStruct((t, D_MODEL), BF16)],
        compiler_params=_params("parallel"),
        name="out_proj_residual",
    )(x, merged, w_o, norm2_g.reshape(1, D_MODEL))


def _ple_kernel(h_ref, y_ref, p_ref, g3_ref, wg_ref, wp_ref, gf_ref, o_ref):
    h = h_ref[...] + y_ref[...]
    n3 = (h * _rms_scale(h) * g3_ref[...]).astype(BF16)
    gate = jax.nn.sigmoid(jnp.dot(n3, wg_ref[...], preferred_element_type=F32))
    pe = jnp.dot(p_ref[...].astype(BF16), wp_ref[...], preferred_element_type=F32)
    h = h + gate * pe
    o_ref[...] = h * _rms_scale(h) * gf_ref[...]


def _ple(h, y, p_emb, norm3_g, w_gate, w_ple, final_g, tm):
    t = h.shape[0]
    tm = min(tm, t)
    rowb = pl.BlockSpec((tm, D_MODEL), lambda i: (i, 0))
    vec = pl.BlockSpec((1, D_MODEL), lambda i: (0, 0))
    return pl.pallas_call(
        _ple_kernel,
        grid=(t // tm,),
        in_specs=[rowb, rowb, pl.BlockSpec((tm, PLE_DIM), lambda i: (i, 0)), vec,
                  pl.BlockSpec((D_MODEL, D_MODEL), lambda i: (0, 0)),
                  pl.BlockSpec((PLE_DIM, D_MODEL), lambda i: (0, 0)), vec],
        out_specs=rowb,
        out_shape=jax.ShapeDtypeStruct((t, D_MODEL), F32),
        compiler_params=_params("parallel"),
        name="ple_gate_final_norm",
    )(h, y, p_emb, norm3_g.reshape(1, D_MODEL), w_gate, w_ple, final_g.reshape(1, D_MODEL))


def _top16_threshold(w):
    for _ in range(PEER_TOPK - 1):
        m = jnp.max(w, axis=0, keepdims=True)
        w = jnp.where(w == m, -jnp.inf, w)
    return jnp.max(w, axis=0, keepdims=True)


def _top16_rows(w):
    rows = []
    for k in range(PEER_TOPK):
        m = jnp.max(w, axis=0, keepdims=True)
        rows.append(m)
        if k + 1 < PEER_TOPK:
            w = jnp.where(w == m, -jnp.inf, w)
    return rows


def _peer_score_kernel(xn_ref, wq_ref, k0_ref, k1_ref, a_ref, b_ref, ea_ref, eb_ref, thr_ref):
    qp = jnp.dot(xn_ref[...], wq_ref[...], preferred_element_type=F32).astype(BF16)
    rank = lax.broadcasted_iota(jnp.int32, (PEER_TOPK, 1), 0)
    n_lead = 4
    for h in range(PEER_HEADS):
        qh = qp[:, h * PEER_KDIM:(h + 1) * PEER_KDIM]
        a = lax.dot_general(k0_ref[h], qh, _NT, preferred_element_type=F32)
        b = lax.dot_general(k1_ref[h], qh, _NT, preferred_element_type=F32)
        ra = _top16_rows(a)
        rb = _top16_rows(b)
        sa = jnp.concatenate(ra, axis=0)
        sb = jnp.concatenate(rb, axis=0)
        sa_tail = jnp.where(rank < n_lead, -jnp.inf, sa)
        cand = jnp.concatenate([ra[i] + sb for i in range(n_lead)] + [rb[i] + sa_tail for i in range(n_lead)], axis=0)
        thr = _top16_threshold(cand)
        vmax = ra[0] + rb[0]
        z = jnp.sum(jnp.where(cand >= thr, jnp.exp(cand - vmax), 0.0), axis=0, keepdims=True)
        a_ref[h] = a
        b_ref[h] = b
        ea_ref[h] = jnp.exp(a - ra[0]) / z
        eb_ref[h] = jnp.exp(b - rb[0])
        thr_ref[h:h + 1, :] = thr


def _peer_scores(xn, wq, k0, k1, tb):
    t = xn.shape[0]
    tb = min(tb, t)
    tab =pl.BlockSpec((PEER_HEADS, N_KEYS, tb), lambda i: (0, 0, i))
    tab_shape = jax.ShapeDtypeStruct((PEER_HEADS, N_KEYS, t), F32)
    full3 = lambda a: pl.BlockSpec(a.shape, lambda i: (0, 0, 0))
    return pl.pallas_call(
        _peer_score_kernel,
        grid=(t // tb,),
        in_specs=[pl.BlockSpec((tb, D_MODEL), lambda i: (i, 0)),
                  pl.BlockSpec(wq.shape, lambda i: (0, 0)), full3(k0), full3(k1)],
        out_specs=[tab, tab, tab, tab, pl.BlockSpec((PEER_HEADS, tb), lambda i: (0, i))],
        out_shape=[tab_shape, tab_shape, tab_shape, tab_shape, jax.ShapeDtypeStruct((PEER_HEADS, t), F32)],
        compiler_params=_params("parallel"),
        name="peer_scores",
    )(xn, wq, k0, k1)


def _peer_dense_kernel(xn_ref, u_ref, vt_ref, a_ref, ea_ref, b_ref, eb_ref, thr_ref, o_ref,
                       at_ref, w_ref, acc_ref, *, tb, eb):
    e = pl.program_id(1)

    @pl.when(e == 0)
    def _():
        acc_ref[...] = jnp.zeros_like(acc_ref)

    at_ref[...] = lax.dot_general(u_ref[...], xn_ref[...], _NT, preferred_element_type=F32)

    for ii in range(eb // N_KEYS):
        rs = slice(ii * N_KEYS, (ii + 1) * N_KEYS)
        for tl in range(tb // LANES):
            ls = slice(tl * LANES, (tl + 1) * LANES)
            gate = jnp.zeros((N_KEYS, LANES), F32)
            for h in range(PEER_HEADS):
                pair = a_ref[h, ii:ii + 1, ls] + b_ref[h, :, ls]
                gate = gate + jnp.where(pair >= thr_ref[h:h + 1, ls], ea_ref[h, ii:ii + 1, ls] * eb_ref[h, :, ls], 0.0)
            at = at_ref[rs, ls]
            gelu = 0.5 * at * (1.0 + lax.erf(at * math.sqrt(0.5)))
            w_ref[rs, ls] = (gate * gelu).astype(BF16)

    acc_ref[...] += jnp.dot(vt_ref[...], w_ref[...], preferred_element_type=F32)

    @pl.when(e == pl.num_programs(1) - 1)
    def _():
        o_ref[...] = acc_ref[...].T


def _peer_dense(xn, u, vt, a, ea, b, eb_tab, thr, tb, eb):
    t = xn.shape[0]
    tb = min(tb, t)
    n_exp = u.shape[0]
    rows = eb // N_KEYS
    col_tab = pl.BlockSpec((PEER_HEADS, N_KEYS, tb), lambda i, e: (0, 0, i))
    row_tab = pl.BlockSpec((PEER_HEADS, rows, tb), lambda i, e: (0, e, i))
    return pl.pallas_call(
        functools.partial(_peer_dense_kernel, tb=tb, eb=eb),
        grid=(t // tb, n_exp // eb),
        in_specs=[
            pl.BlockSpec((tb, D_MODEL), lambda i, e: (i, 0)),
            pl.BlockSpec((eb, D_MODEL), lambda i, e: (e, 0)),
            pl.BlockSpec((D_MODEL, eb), lambda i, e: (0, e)),
            row_tab, row_tab, col_tab, col_tab,
            pl.BlockSpec((PEER_HEADS, tb), lambda i, e: (0, i)),
        ],
        out_specs=pl.BlockSpec((tb, D_MODEL), lambda i, e: (i, 0)),
        out_shape=jax.ShapeDtypeStruct((t, D_MODEL), F32),
        scratch_shapes=[
            pltpu.VMEM((eb, tb), F32),
            pltpu.VMEM((eb, tb), BF16),
            pltpu.VMEM((D_MODEL, tb), F32),
        ],
        compiler_params=_params("parallel", "arbitrary"),
        name="peer_dense",
    )(xn, u, vt, a, ea, b, eb_tab, thr)


def _rope_swap(w):
    half = w.shape[-1] // 2
    return jnp.concatenate([-w[..., half:], w[..., :half]], axis=-1)


def _prepare_weights(w_in, w_uq, w_ukv, peer_keys):
    c_rg = 2 * RET_QK_W + 2 * RET_V_W
    c_ckv = c_rg + Q_LORA
    c_kr = c_ckv + KV_LORA
    c_ga = c_kr + QK_ROPE
    w_big = jnp.concatenate([w_in[:, :c_rg], w_in[:, c_ga:]], axis=1).astype(BF16)
    w_kr = w_in[:, c_kr:c_ga]
    w_small = jnp.concatenate([w_in[:, c_rg:c_kr], w_kr, _rope_swap(w_kr)], axis=1).astype(BF16)
    q_rope = w_uq[:, :, QK_NOPE:]
    wq = jnp.concatenate([w_uq[:, :, :QK_NOPE], q_rope, _rope_swap(q_rope)], axis=-1)
    wq = wq.reshape(Q_LORA, MLA_HEADS * QK_PAD).astype(BF16)
    wk = w_ukv[:, :, :QK_NOPE].reshape(KV_LORA, MLA_HEADS * QK_NOPE).astype(BF16)
    wv = w_ukv[:, :, QK_NOPE:].reshape(KV_LORA, MLA_HEADS * V_HEAD).astype(BF16)
    w_uk_t = jnp.transpose(w_ukv[:, :, :QK_NOPE], (1, 2, 0)).astype(BF16)
    w_uv = jnp.transpose(w_ukv[:, :, QK_NOPE:], (1, 0, 2)).astype(BF16)
    zeros = jnp.zeros((PEER_HEADS, N_KEYS, PEER_HALF), F32)
    k0 = jnp.concatenate([peer_keys[:, 0], zeros], axis=-1).astype(BF16)
    k1 = jnp.concatenate([zeros, peer_keys[:, 1]], axis=-1).astype(BF16)
    return w_big, w_small, wq, wk, wv, w_uk_t, w_uv, k0, k1


def _q_rope_tables(pos):
    cos, sin = _rope_tables(pos, QK_ROPE // 2)
    zeros = jnp.zeros((pos.shape[0], LANES - QK_ROPE), F32)
    return jnp.concatenate([cos, cos, zeros], axis=1), jnp.concatenate([sin, sin, zeros], axis=1)


def kernel(x_prompt, x_sample, cache_ckv, cache_krope, state_ret, page_table, p_prompt, p_sample, norm1_g, w_in, q_norm_g, w_uq, kv_norm_g, w_ukv, w_ret_o, w_mla_o, w_o, norm2_g, peer_wq, peer_keys, peer_u, peer_v, norm3_g, w_ple_gate, w_ple, final_norm_g):
    b_p, l_p, _ = x_prompt.shape
    b_s, l_s, _ = x_sample.shape
    depth = w_in.shape[0]
    assert depth == 1
    past_len = page_table.shape[1] * PAGE_SIZE
    t_p = b_p * l_p
    t_s = b_s * l_s

    w_big, w_small, wq, wk, wv, w_uk_t, w_uv, k0, k1 = _prepare_weights(w_in[0], w_uq[0], w_ukv[0], peer_keys[0])
    w_ret_o_b = w_ret_o[0].astype(BF16)
    w_mla_o_b = w_mla_o[0].astype(BF16)
    w_o_b = w_o[0].astype(BF16)
    peer_wq_b = peer_wq[0].astype(BF16)
    u_b = peer_u[0].astype(BF16)
    vt_b = peer_v[0].astype(BF16).T
    w_gate_b = w_ple_gate[0].astype(BF16)
    w_ple_b = w_ple[0].astype(BF16)

    xp = x_prompt.reshape(t_p, D_MODEL)
    xs = x_sample.reshape(t_s, D_MODEL)
    pos_p = jnp.arange(l_p, dtype=jnp.int32)
    pos_s = past_len + jnp.arange(l_s, dtype=jnp.int32)

    big_p = _norm_matmul(xp, norm1_g[0], w_big, BF16, tm=512, tn=2048)
    big_s = _norm_matmul(xs, norm1_g[0], w_big, BF16, tm=512, tn=2048)
    small_p = _norm_matmul(xp, norm1_g[0], w_small, F32, tm=512, tn=SMALL_W)
    small_s = _norm_matmul(xs, norm1_g[0], w_small, F32, tm=512, tn=SMALL_W)

    ro_p, state_p = _retention_prompt(big_p, b_p, l_p, rb=256)
    ro_s, state_s = _retention_sample(big_s, state_ret[0], b_s, l_s, past_len)

    cos_p, sin_p = _q_rope_tables(pos_p)
    cos_p = jnp.tile(cos_p, (b_p, 1))
    sin_p = jnp.tile(sin_p, (b_p, 1))
    q_p, ckv_p, kr_p, k_p, v_p = _mla_proj(small_p, q_norm_g[0], kv_norm_g[0], wq, wk, wv, cos_p, sin_p, 256, True)
    mo_p = _flash_attention(q_p, k_p, v_p, b_p, l_p, tq=512)

    cos_s, sin_s = _q_rope_tables(pos_s)
    cos_s = jnp.tile(cos_s, (b_s, 1))
    sin_s = jnp.tile(sin_s, (b_s, 1))
    q_s, ckv_s, kr_s = _mla_proj(small_s, q_norm_g[0], kv_norm_g[0], wq, wk, wv, cos_s, sin_s, 256, False)
    q_lat = _head_matmul(q_s, w_uk_t, 2, BF16).reshape(b_s, l_s * MLA_HEADS, KV_LORA)
    q_rope = q_s.reshape(t_s, MLA_HEADS, QK_PAD)[:, :, QK_NOPE:QK_NOPE + QK_ROPE].reshape(b_s, l_s * MLA_HEADS, QK_ROPE)
    o_lat = _decode_attention(q_lat, q_rope, cache_ckv, cache_krope, page_table,
                              ckv_s.reshape(b_s, l_s, KV_LORA), kr_s.reshape(b_s, l_s, QK_ROPE), pg=8)
    mo_s = _head_matmul(o_lat.reshape(t_s, MLA_HEADS * KV_LORA), w_uv, 1, BF16)

    outs = []
    for x, big, ro, mo, p_emb in ((xp, big_p, ro_p, mo_p, p_prompt[0]), (xs, big_s, ro_s, mo_s, p_sample[0])):
        t = x.shape[0]
        merged = _merge(ro, mo, w_ret_o_b, w_mla_o_b, big, tm=512, tn=512)
        h, xn = _oproj(x, merged, w_o_b, norm2_g[0], tm=256)
        a, b, ea, eb_tab, thr = _peer_scores(xn, peer_wq_b, k0, k1, tb=256)
        y = _peer_dense(xn, u_b, vt_b, a, ea, b, eb_tab, thr, tb=512, eb=1024)
        outs.append(_ple(h, y, p_emb.reshape(t, PLE_DIM), norm3_g[0], w_gate_b, w_ple_b, final_norm_g, tm=256))

    y_prompt = outs[0].reshape(b_p, l_p, D_MODEL)
    y_sample = outs[1].reshape(b_s, l_s, D_MODEL)
    return (y_prompt, y_sample,
            ckv_p.reshape(1, b_p, l_p, KV_LORA), kr_p.reshape(1, b_p, l_p, QK_ROPE), state_p[None],
            ckv_s.reshape(1, b_s, l_s, KV_LORA), kr_s.reshape(1, b_s, l_s, QK_ROPE), state_s[None])
```

```python
import functools
import math

import jax
import jax.numpy as jnp
from jax import lax
from jax.experimental import pallas as pl
from jax.experimental.pallas import tpu as pltpu

F32 = jnp.float32
BF16 = jnp.bfloat16

D_MODEL = 2048
RET_HEADS = 8
RET_DK = 128
RET_DV = 256
RET_CHUNK = 128
RET_QK_W = RET_HEADS * RET_DK
RET_V_W = RET_HEADS * RET_DV
MLA_HEADS = 16
Q_LORA = 768
KV_LORA = 512
QK_NOPE = 128
QK_ROPE = 64
V_HEAD = 128
MLA_SCALE = (QK_NOPE + QK_ROPE) ** -0.5
ROPE_THETA = 10000.0
PEER_HEADS = 8
N_KEYS = 128
PEER_KDIM = 128
PEER_HALF = PEER_KDIM // 2
PEER_TOPK = 16
PLE_DIM = 256
EPS = 1e-6
PAGE_SIZE = 128

LANES = 128
QK_PAD = 2 * LANES
SMALL_W = Q_LORA + KV_LORA + LANES
BIG_W = 2 * RET_QK_W + 2 * RET_V_W + 2 * D_MODEL
VMEM_LIMIT = 56 * 1024 * 1024
NEG_BIG = -1e30

_NT = (((1,), (1,)), ((), ()))
_TN = (((0,), (0,)), ((), ()))


def _params(*sem):
    return pltpu.CompilerParams(dimension_semantics=sem, vmem_limit_bytes=VMEM_LIMIT)


def _rms_scale(x):
    return lax.rsqrt(jnp.mean(x * x, axis=-1, keepdims=True) + EPS)


def _half_swap(x):
    return pltpu.roll(x, LANES // 2, 1)


def _norm_matmul_kernel(x_ref, g_ref, w_ref, o_ref, xn_ref):
    @pl.when(pl.program_id(1) == 0)
    def _():
        x = x_ref[...]
        xn_ref[...] = (x * _rms_scale(x) * g_ref[...]).astype(BF16)

    o_ref[...] = jnp.dot(xn_ref[...], w_ref[...], preferred_element_type=F32).astype(o_ref.dtype)


def _norm_matmul(x, g, w, out_dtype, tm, tn):
    t, d = x.shape
    n = w.shape[1]
    tm = min(tm, t)
    return pl.pallas_call(
        _norm_matmul_kernel,
        grid=(t // tm, n // tn),
        in_specs=[
            pl.BlockSpec((tm, d), lambda i, j: (i, 0)),
            pl.BlockSpec((1, d), lambda i, j: (0, 0)),
            pl.BlockSpec((d, tn), lambda i, j: (0, j)),
        ],
        out_specs=pl.BlockSpec((tm, tn), lambda i, j: (i, j)),
        out_shape=jax.ShapeDtypeStruct((t, n), out_dtype),
        scratch_shapes=[pltpu.VMEM((tm, d), BF16)],
        compiler_params=_params("parallel", "arbitrary"),
        name="norm_matmul",
    )(x, g.reshape(1, d), w)


def _retention_consts(c):
    lg = jnp.log(1.0 - 2.0 ** (-5.0 - jnp.arange(RET_HEADS, dtype=F32)))
    idx = jnp.arange(c, dtype=F32)
    diff = idx[:, None] - idx[None, :]
    causal = diff >= 0
    dmat = jnp.where(causal[None], jnp.exp(lg[:, None, None] * jnp.where(causal, diff, 0.0)[None]), 0.0)
    q_dec = jnp.exp(lg[:, None] * (idx[None, :] + 1.0))
    k_dec = jnp.exp(lg[:, None] * (c - 1.0 - idx[None, :]))
    s_dec = jnp.exp(lg * c)
    qd = jnp.broadcast_to(q_dec[:, :, None], (RET_HEADS, c, RET_DK))
    kd = jnp.broadcast_to(k_dec[:, :, None], (RET_HEADS, c, RET_DK))
    sd = jnp.broadcast_to(s_dec[:, None, None], (RET_HEADS, 8, RET_DV))
    return dmat, qd, kd, sd


def _rope_tables(pos, half):
    inv = ROPE_THETA ** (-jnp.arange(half, dtype=F32) / half)
    ang = pos.astype(F32)[:, None] * inv[None, :]
    return jnp.cos(ang), jnp.sin(ang)


def _ret_head(q, k, v, s, dmat, qd, kd, sd):
    qb = q.astype(BF16)
    kb = k.astype(BF16)
    att = lax.dot_general(qb, kb, _NT, preferred_element_type=F32) * dmat
    o = jnp.dot(att.astype(BF16), v, preferred_element_type=F32)
    o = o + jnp.dot((q * qd).astype(BF16), s.astype(BF16), preferred_element_type=F32)
    s_new = s * sd + lax.dot_general((k * kd).astype(BF16), v, _TN, preferred_element_type=F32)
    return o, s_new


def _ret_prompt_kernel(rq_ref, rk_ref, rv_ref, rg_ref, cos_ref, sin_ref, dmat_ref, qd_ref, kd_ref, sd_ref,
                       o_ref, s_ref, *, n_sub):
    @pl.when(pl.program_id(1) == 0)
    def _():
        s_ref[...] = jnp.zeros_like(s_ref)

    c = RET_CHUNK
    for sub in range(n_sub):
        rows = slice(sub * c, (sub + 1) * c)
        cos = cos_ref[rows, :]
        sin = sin_ref[rows, :]
        for h in range(RET_HEADS):
            qk = slice(h * RET_DK, (h + 1) * RET_DK)
            vv = slice(h * RET_DV, (h + 1) * RET_DV)
            q = rq_ref[rows, qk].astype(F32)
            k = rk_ref[rows, qk].astype(F32)
            q = q * cos + _half_swap(q) * sin
            k = (k * cos + _half_swap(k) * sin) * (RET_DK ** -0.5)
            o, s_new = _ret_head(q, k, rv_ref[rows, vv], s_ref[0, h], dmat_ref[h], qd_ref[h], kd_ref[h],
                                 sd_ref[h, 0:1, :])
            s_ref[0, h] = s_new
            rg = rg_ref[rows, vv].astype(F32)
            o_ref[rows, vv] = (o * _rms_scale(o) * (rg * jax.nn.sigmoid(rg))).astype(BF16)


def _retention_prompt(proj, b, l, rb):
    rb = min(rb, l)
    n_r = l // rb
    cos, sin = _rope_tables(jnp.arange(l, dtype=jnp.int32), RET_DK // 2)
    cos2 = jnp.concatenate([cos, cos], axis=1)
    sin2 = jnp.concatenate([-sin, sin], axis=1)
    dmat, qd, kd, sd = _retention_consts(RET_CHUNK)
    row = lambda bi, r: bi * n_r + r
    const3 = lambda bi, r: (0, 0, 0)
    return pl.pallas_call(
        functools.partial(_ret_prompt_kernel, n_sub=rb // RET_CHUNK),
        grid=(b, n_r),
        in_specs=[
            pl.BlockSpec((rb, RET_QK_W), lambda bi, r: (row(bi, r), 0)),
            pl.BlockSpec((rb, RET_QK_W), lambda bi, r: (row(bi, r), 1)),
            pl.BlockSpec((rb, RET_V_W), lambda bi, r: (row(bi, r), 1)),
            pl.BlockSpec((rb, RET_V_W), lambda bi, r: (row(bi, r), 2)),
            pl.BlockSpec((rb, RET_DK), lambda bi, r: (r, 0)),
            pl.BlockSpec((rb, RET_DK), lambda bi, r: (r, 0)),
            pl.BlockSpec(dmat.shape, const3),
            pl.BlockSpec(qd.shape, const3),
            pl.BlockSpec(kd.shape, const3),
            pl.BlockSpec(sd.shape, const3),
        ],
        out_specs=[
            pl.BlockSpec((rb, RET_V_W), lambda bi, r: (row(bi, r), 0)),
            pl.BlockSpec((1, RET_HEADS, RET_DK, RET_DV), lambda bi, r: (bi, 0, 0, 0)),
        ],
        out_shape=[
            jax.ShapeDtypeStruct((b * l, RET_V_W), BF16),
            jax.ShapeDtypeStruct((b, RET_HEADS, RET_DK, RET_DV), F32),
        ],
        compiler_params=_params("parallel", "arbitrary"),
        name="retention_prompt",
    )(proj, proj, proj, proj, cos2, sin2, dmat, qd, kd, sd)


def _ret_sample_kernel(rq_ref, rk_ref, rv_ref, rg_ref, cos_ref, sin_ref, dmat_ref, qd_ref, kd_ref, sd_ref, s0_ref,
                       o_ref, s_ref, *, l):
    cos = cos_ref[...]
    sin = sin_ref[...]
    first = lax.broadcasted_iota(jnp.int32, (2 * l, 1), 0) < l
    for h in range(RET_HEADS):
        qk = slice(h * RET_DK, (h + 1) * RET_DK)
        vv = slice(h * RET_DV, (h + 1) * RET_DV)
        q = rq_ref[:, qk].astype(F32)
        k = rk_ref[:, qk].astype(F32)
        q = q * cos + _half_swap(q) * sin
        k = (k * cos + _half_swap(k) * sin) * (RET_DK ** -0.5)
        v = rv_ref[:, vv]
        att = lax.dot_general(q.astype(BF16), k.astype(BF16), _NT, preferred_element_type=F32) * dmat_ref[h]
        o = jnp.dot(att.astype(BF16), v, preferred_element_type=F32)
        qdb = (q * qd_ref[h]).astype(BF16)
        kdk = k * kd_ref[h]
        sd = sd_ref[h, 0:1, :]
        s_a = s0_ref[0, h]
        s_b = s0_ref[1, h]
        o = o + jnp.where(first,
                          jnp.dot(qdb, s_a.astype(BF16), preferred_element_type=F32),
                          jnp.dot(qdb, s_b.astype(BF16), preferred_element_type=F32))
        k_a = jnp.where(first, kdk, 0.0).astype(BF16)
        k_b = jnp.where(first, 0.0, kdk).astype(BF16)
        s_ref[0, h] = s_a * sd + lax.dot_general(k_a, v, _TN, preferred_element_type=F32)
        s_ref[1, h] = s_b * sd + lax.dot_general(k_b, v, _TN, preferred_element_type=F32)
        rg = rg_ref[:, vv].astype(F32)
        o_ref[:, vv] = (o * _rms_scale(o) * (rg * jax.nn.sigmoid(rg))).astype(BF16)


def _retention_sample(proj, state, b, l, pos0):
    assert b % 2 == 0
    cos, sin = _rope_tables(pos0 + jnp.arange(l, dtype=jnp.int32), RET_DK // 2)
    cos2 = jnp.tile(jnp.concatenate([cos, cos], axis=1), (2, 1))
    sin2 = jnp.tile(jnp.concatenate([-sin, sin], axis=1), (2, 1))
    dmat, qd, kd, sd = _retention_consts(l)
    zeros = jnp.zeros_like(dmat)
    dmat2 = jnp.concatenate([jnp.concatenate([dmat, zeros], axis=2), jnp.concatenate([zeros, dmat], axis=2)], axis=1)
    qd2 = jnp.tile(qd, (1, 2, 1))
    kd2 = jnp.tile(kd, (1, 2, 1))
    rows = 2 * l
    const2 = lambda i: (0, 0)
    const3 = lambda i: (0, 0, 0)
    return pl.pallas_call(
        functools.partial(_ret_sample_kernel, l=l),
        grid=(b // 2,),
        in_specs=[
            pl.BlockSpec((rows, RET_QK_W), lambda i: (i, 0)),
            pl.BlockSpec((rows, RET_QK_W), lambda i: (i, 1)),
            pl.BlockSpec((rows, RET_V_W), lambda i: (i, 1)),
            pl.BlockSpec((rows, RET_V_W), lambda i: (i, 2)),
            pl.BlockSpec((rows, RET_DK), const2),
            pl.BlockSpec((rows, RET_DK), const2),
            pl.BlockSpec(dmat2.shape, const3),
            pl.BlockSpec(qd2.shape, const3),
            pl.BlockSpec(kd2.shape, const3),
            pl.BlockSpec(sd.shape, const3),
            pl.BlockSpec((2, RET_HEADS, RET_DK, RET_DV), lambda i: (i, 0, 0, 0)),
        ],
        out_specs=[
            pl.BlockSpec((rows, RET_V_W), lambda i: (i, 0)),
            pl.BlockSpec((2, RET_HEADS, RET_DK, RET_DV), lambda i: (i, 0, 0, 0)),
        ],
        out_shape=[
            jax.ShapeDtypeStruct((b * l, RET_V_W), BF16),
            jax.ShapeDtypeStruct((b, RET_HEADS, RET_DK, RET_DV), F32),
        ],
        compiler_params=_params("parallel"),
        name="retention_sample",
    )(proj, proj, proj, proj, cos2, sin2, dmat2, qd2, kd2, sd, state)


def _mla_proj_kernel(x_ref, qg_ref, kvg_ref, wq_ref, wk_ref, wv_ref, cos_ref, sin_ref,
                     q_ref, ckv_ref, kr_ref, *kv_refs):
    cos = cos_ref[...]
    sin = sin_ref[...]
    cq = x_ref[:, 0:Q_LORA]
    cqn = (cq * _rms_scale(cq) * qg_ref[...]).astype(BF16)
    q = jnp.dot(cqn, wq_ref[...], preferred_element_type=F32)
    for h in range(MLA_HEADS):
        lo = h * QK_PAD
        q_ref[:, lo:lo + LANES] = q[:, lo:lo + LANES].astype(BF16)
        r = q[:, lo + LANES:lo + QK_PAD]
        q_ref[:, lo + LANES:lo + QK_PAD] = (r * cos + _half_swap(r) * sin).astype(BF16)
    ckv = x_ref[:, Q_LORA:Q_LORA + KV_LORA]
    ckvn = ckv * _rms_scale(ckv) * kvg_ref[...]
    ckv_ref[...] = ckvn
    krx = x_ref[:, Q_LORA + KV_LORA:SMALL_W]
    krr = krx * cos + _half_swap(krx) * sin
    kr_ref[...] = krr[:, 0:QK_ROPE]
    if kv_refs:
        k_ref, v_ref = kv_refs
        cb = ckvn.astype(BF16)
        kn = jnp.dot(cb, wk_ref[...], preferred_element_type=F32)
        krb = krr.astype(BF16)
        for h in range(MLA_HEADS):
            lo = h * QK_PAD
            k_ref[:, lo:lo + LANES] = kn[:, h * QK_NOPE:(h + 1) * QK_NOPE].astype(BF16)
            k_ref[:, lo + LANES:lo + QK_PAD] = krb
        v_ref[...] = jnp.dot(cb, wv_ref[...], preferred_element_type=F32).astype(BF16)


def _mla_proj(small, q_norm_g, kv_norm_g, wq, wk, wv, cosq, sinq, tm, with_kv):
    t = small.shape[0]
    tm = min(tm, t)
    rowb = lambda n: pl.BlockSpec((tm, n), lambda i: (i, 0))
    full = lambda a: pl.BlockSpec(a.shape, lambda i: (0, 0))
    qg = q_norm_g.reshape(1, Q_LORA)
    kvg = kv_norm_g.reshape(1, KV_LORA)
    out_specs = [rowb(MLA_HEADS * QK_PAD), rowb(KV_LORA), rowb(QK_ROPE)]
    out_shape = [
        jax.ShapeDtypeStruct((t, MLA_HEADS * QK_PAD), BF16),
        jax.ShapeDtypeStruct((t, KV_LORA), F32),
        jax.ShapeDtypeStruct((t, QK_ROPE), F32),
    ]
    if with_kv:
        out_specs += [rowb(MLA_HEADS * QK_PAD), rowb(MLA_HEADS * V_HEAD)]
        out_shape += [
            jax.ShapeDtypeStruct((t, MLA_HEADS * QK_PAD), BF16),
            jax.ShapeDtypeStruct((t, MLA_HEADS * V_HEAD), BF16),
        ]
    return pl.pallas_call(
        _mla_proj_kernel,
        grid=(t // tm,),
        in_specs=[rowb(SMALL_W), full(qg), full(kvg), full(wq), full(wk), full(wv), rowb(LANES), rowb(LANES)],
        out_specs=out_specs,
        out_shape=out_shape,
        compiler_params=_params("parallel"),
        name="mla_proj",
    )(small, qg, kvg, wq, wk, wv, cosq, sinq)


FLASH_HEADS = 2


def _flash_kernel(q_ref, k_ref, v_ref, o_ref, *, tq):
    qi = pl.program_id(2)
    qs = [q_ref[:, g * QK_PAD:(g + 1) * QK_PAD] for g in range(FLASH_HEADS)]

    def scores(j, g):
        off = pl.multiple_of(j * tq, tq)
        k = k_ref[pl.ds(off, tq), g * QK_PAD:(g + 1) * QK_PAD]
        v = v_ref[pl.ds(off, tq), g * V_HEAD:(g + 1) * V_HEAD]
        s = lax.dot_general(qs[g], k, _NT, preferred_element_type=F32) * MLA_SCALE
        return s, v

    def update(carry, s, v):
        m, l, acc = carry
        m_new = jnp.maximum(m, jnp.max(s, axis=-1, keepdims=True))
        corr = jnp.exp(m - m_new)
        p = jnp.exp(s - m_new)
        l = l * corr + jnp.sum(p, axis=-1, keepdims=True)
        acc = acc * corr + jnp.dot(p.astype(BF16), v, preferred_element_type=F32)
        return m_new, l, acc

    def body(j, carries):
        return tuple(update(carries[g], *scores(j, g)) for g in range(FLASH_HEADS))

    init = (jnp.full((tq, 1), NEG_BIG, F32), jnp.zeros((tq, 1), F32), jnp.zeros((tq, V_HEAD), F32))
    carries = lax.fori_loop(0, qi, body, (init,) * FLASH_HEADS)
    row = lax.broadcasted_iota(jnp.int32, (tq, tq), 0)
    col = lax.broadcasted_iota(jnp.int32, (tq, tq), 1)
    for g in range(FLASH_HEADS):
        s, v = scores(qi, g)
        _, l, acc = update(carries[g], jnp.where(col <= row, s, -jnp.inf), v)
        o_ref[:, g * V_HEAD:(g + 1) * V_HEAD] = (acc / l).astype(BF16)


def _flash_attention(q, k, v, b, l, tq):
    tq = min(tq, l)
    n_q = l // tq
    qw = FLASH_HEADS * QK_PAD
    vw = FLASH_HEADS * V_HEAD
    return pl.pallas_call(
        functools.partial(_flash_kernel, tq=tq),
        grid=(b, MLA_HEADS // FLASH_HEADS, n_q),
        in_specs=[
            pl.BlockSpec((tq, qw), lambda bi, h, i: (bi * n_q + i, h)),
            pl.BlockSpec((l, qw), lambda bi, h, i: (bi, h)),
            pl.BlockSpec((l, vw), lambda bi, h, i: (bi, h)),
        ],
        out_specs=pl.BlockSpec((tq, vw), lambda bi, h, i: (bi * n_q + i, h)),
        out_shape=jax.ShapeDtypeStruct((b * l, MLA_HEADS * V_HEAD), BF16),
        compiler_params=_params("parallel", "parallel", "arbitrary"),
        name="mla_prompt_attention",
    )(q, k, v)


def _head_matmul_kernel(x_ref, w_ref, o_ref):
    o_ref[...] = jnp.dot(x_ref[...], w_ref[0], preferred_element_type=F32).astype(o_ref.dtype)


def _head_matmul(x, w, col_stride, out_dtype):
    m = x.shape[0]
    heads, kd, n = w.shape
    return pl.pallas_call(
        _head_matmul_kernel,
        grid=(heads,),
        in_specs=[
            pl.BlockSpec((m, kd), lambda h: (0, h * col_stride)),
            pl.BlockSpec((1, kd, n), lambda h: (h, 0, 0)),
        ],
        out_specs=pl.BlockSpec((m, n), lambda h: (0, h)),
        out_shape=jax.ShapeDtypeStruct((m, heads * n), out_dtype),
        compiler_params=_params("parallel"),
        name="head_matmul",
    )(x, w)


def _decode_kernel(pt_ref, ql_ref, qr_ref, *refs, pg, l_new):
    ckv_refs = refs[:pg]
    kr_refs = refs[pg:2 * pg]
    cn_ref, kn_ref, o_ref, m_ref, l_ref, acc_ref, kc_ref, krc_ref = refs[2 * pg:]
    g = pl.program_id(1)
    rows = ql_ref.shape[1]

    @pl.when(g == 0)
    def _():
        m_ref[...] = jnp.full_like(m_ref, NEG_BIG)
        l_ref[...] = jnp.zeros_like(l_ref)
        acc_ref[...] = jnp.zeros_like(acc_ref)

    ql = ql_ref[0]
    qr = qr_ref[0]

    def absorb(s, vals):
        m = m_ref[...]
        m_new = jnp.maximum(m, jnp.max(s, axis=-1, keepdims=True))
        corr = jnp.exp(m - m_new)
        p = jnp.exp(s - m_new[:, 0:1])
        l_ref[...] = l_ref[...] * corr + jnp.sum(p, axis=-1, keepdims=True)
        pv = acc_ref[...] * corr[:, 0:1]
        for i, val in enumerate(vals):
            w = val.shape[0]
            pv = pv + jnp.dot(p[:, i * w:(i + 1) * w].astype(BF16), val, preferred_element_type=F32)
        acc_ref[...] = pv
        m_ref[...] = m_new

    def score(ckv, kr):
        return (lax.dot_general(ql, ckv, _NT, preferred_element_type=F32)
                + lax.dot_general(qr, kr, _NT, preferred_element_type=F32)) * MLA_SCALE

    for i in range(pg):
        rs = slice(i * PAGE_SIZE, (i + 1) * PAGE_SIZE)
        kc_ref[rs, :] = ckv_refs[i][0, 0].astype(BF16)
        krc_ref[rs, :] = kr_refs[i][0, 0].astype(BF16)
    kc = kc_ref[...]
    absorb(score(kc, krc_ref[...]), [kc])

    @pl.when(g == pl.num_programs(1) - 1)
    def _():
        cn = cn_ref[0].astype(BF16)
        s_new = score(cn, kn_ref[0].astype(BF16))
        tok = lax.broadcasted_iota(jnp.int32, s_new.shape, 0) // MLA_HEADS
        col = lax.broadcasted_iota(jnp.int32, s_new.shape, 1)
        absorb(jnp.where((col <= tok) & (col < l_new), s_new, -jnp.inf), [cn])
        o_ref[0] = (acc_ref[...] / l_ref[:, 0:1]).astype(BF16)


def _page_index(b, g, pt, *, i, pg):
    return (0, pt[b, g * pg + i], 0, 0)


def _decode_attention(q_lat, q_rope, cache_ckv, cache_krope, page_table, ckv_new, kr_new, pg):
    b, rows, _ = q_lat.shape
    n_pages = page_table.shape[1]
    pg = min(pg, n_pages)
    l_new = ckv_new.shape[1]
    pad = 16 - l_new
    cn = jnp.pad(ckv_new, ((0, 0), (0, pad), (0, 0)))
    kn = jnp.pad(kr_new, ((0, 0), (0, pad), (0, 0)))
    per_b = lambda n: pl.BlockSpec((1, n[0], n[1]), lambda bi, g, pt: (bi, 0, 0))
    page_specs = (
        [pl.BlockSpec((1, 1, PAGE_SIZE, KV_LORA), functools.partial(_page_index, i=i, pg=pg)) for i in range(pg)]
        + [pl.BlockSpec((1, 1, PAGE_SIZE, QK_ROPE), functools.partial(_page_index, i=i, pg=pg)) for i in range(pg)])
    grid_spec = pltpu.PrefetchScalarGridSpec(
        num_scalar_prefetch=1,
        grid=(b, n_pages // pg),
        in_specs=[per_b((rows, KV_LORA)), per_b((rows, QK_ROPE))] + page_specs
        + [per_b((16, KV_LORA)), per_b((16, QK_ROPE))],
        out_specs=per_b((rows, KV_LORA)),
        scratch_shapes=[
            pltpu.VMEM((rows, LANES), F32),
            pltpu.VMEM((rows, LANES), F32),
            pltpu.VMEM((rows, KV_LORA), F32),
            pltpu.VMEM((pg * PAGE_SIZE, KV_LORA), BF16),
            pltpu.VMEM((pg * PAGE_SIZE, QK_ROPE), BF16),
        ],
    )
    return pl.pallas_call(
        functools.partial(_decode_kernel, pg=pg, l_new=l_new),
        grid_spec=grid_spec,
        out_shape=jax.ShapeDtypeStruct((b, rows, KV_LORA), BF16),
        compiler_params=_params("parallel", "arbitrary"),
        name="mla_decode_attention",
    )(page_table, q_lat, q_rope, *([cache_ckv] * pg), *([cache_krope] * pg), cn, kn)


def _merge_kernel(ro_ref, mo_ref, wr_ref, wm_ref, ga_ref, gb_ref, o_ref):
    t1 = jnp.dot(ro_ref[...], wr_ref[...], preferred_element_type=F32)
    t2 = jnp.dot(mo_ref[...], wm_ref[...], preferred_element_type=F32)
    ga = jax.nn.sigmoid(ga_ref[...].astype(F32))
    gb = jax.nn.sigmoid(gb_ref[...].astype(F32))
    o_ref[...] = (ga * t1 + gb * t2).astype(BF16)


def _merge(ro, mo, w_ret_o, w_mla_o, proj, tm, tn):
    t = ro.shape[0]
    tm = min(tm, t)
    ga0 = (2 * RET_QK_W + 2 * RET_V_W) // tn
    gb0 = ga0 + D_MODEL // tn
    return pl.pallas_call(
        _merge_kernel,
        grid=(t // tm, D_MODEL // tn),
        in_specs=[
            pl.BlockSpec((tm, RET_V_W), lambda i, j: (i, 0)),
            pl.BlockSpec((tm, MLA_HEADS * V_HEAD), lambda i, j: (i, 0)),
            pl.BlockSpec((RET_V_W, tn), lambda i, j: (0, j)),
            pl.BlockSpec((MLA_HEADS * V_HEAD, tn), lambda i, j: (0, j)),
            pl.BlockSpec((tm, tn), lambda i, j: (i, ga0 + j)),
            pl.BlockSpec((tm, tn), lambda i, j: (i, gb0 + j)),
        ],
        out_specs=pl.BlockSpec((tm, tn), lambda i, j: (i, j)),
        out_shape=jax.ShapeDtypeStruct((t, D_MODEL), BF16),
        compiler_params=_params("parallel", "arbitrary"),
        name="branch_merge",
    )(ro, mo, w_ret_o, w_mla_o, proj, proj)


def _oproj_kernel(x_ref, m_ref, w_ref, g_ref, h_ref, xn_ref):
    h = x_ref[...] + jnp.dot(m_ref[...], w_ref[...], preferred_element_type=F32)
    h_ref[...] = h
    xn_ref[...] = (h * _rms_scale(h) * g_ref[...]).astype(BF16)


def _oproj(x, merged, w_o, norm2_g, tm):
    t = x.shape[0]
    tm = min(tm, t)
    rowb = pl.BlockSpec((tm, D_MODEL), lambda i: (i, 0))
    return pl.pallas_call(
        _oproj_kernel,
        grid=(t // tm,),
        in_specs=[rowb, rowb, pl.BlockSpec((D_MODEL, D_MODEL), lambda i: (0, 0)),
                  pl.BlockSpec((1, D_MODEL), lambda i: (0, 0))],
        out_specs=[rowb, rowb],
        out_shape=[jax.ShapeDtypeStruct((t, D_MODEL), F32), jax.ShapeDtypeStruct((t, D_MODEL), BF16)],
        compiler_params=_params("parallel"),
        name="out_proj_residual",
    )(x, merged, w_o, norm2_g.reshape(1, D_MODEL))


def _ple_kernel(h_ref, y_ref, p_ref, g3_ref, wg_ref, wp_ref, gf_ref, o_ref):
    h = h_ref[...] + y_ref[...]
    n3 = (h * _rms_scale(h) * g3_ref[...]).astype(BF16)
    gate = jax.nn.sigmoid(jnp.dot(n3, wg_ref[...], preferred_element_type=F32))
    pe = jnp.dot(p_ref[...].astype(BF16), wp_ref[...], preferred_element_type=F32)
    h = h + gate * pe
    o_ref[...] = h * _rms_scale(h) * gf_ref[...]


def _ple(h, y, p_emb, norm3_g, w_gate, w_ple, final_g, tm):
    t = h.shape[0]
    tm = min(tm, t)
    rowb = pl.BlockSpec((tm, D_MODEL), lambda i: (i, 0))
    vec = pl.BlockSpec((1, D_MODEL), lambda i: (0, 0))
    return pl.pallas_call(
        _ple_kernel,
        grid=(t // tm,),
        in_specs=[rowb, rowb, pl.BlockSpec((tm, PLE_DIM), lambda i: (i, 0)), vec,
                  pl.BlockSpec((D_MODEL, D_MODEL), lambda i: (0, 0)),
                  pl.BlockSpec((PLE_DIM, D_MODEL), lambda i: (0, 0)), vec],
        out_specs=rowb,
        out_shape=jax.ShapeDtypeStruct((t, D_MODEL), F32),
        compiler_params=_params("parallel"),
        name="ple_gate_final_norm",
    )(h, y, p_emb, norm3_g.reshape(1, D_MODEL), w_gate, w_ple, final_g.reshape(1, D_MODEL))


def _top_rows(w, n):
    rows = []
    for k in range(n):
        m = jnp.max(w, axis=0, keepdims=True)
        rows.append(m)
        if k + 1 < n:
            w = jnp.where(w == m, -jnp.inf, w)
    return rows


def _peer_score_kernel(xn_ref, wq_ref, k0_ref, k1_ref, d_ref, ea_ref, b_ref, eb_ref):
    qp = jnp.dot(xn_ref[...], wq_ref[...], preferred_element_type=F32).astype(BF16)
    rank = lax.broadcasted_iota(jnp.int32, (PEER_TOPK, 1), 0)
    n_lead = 4
    for h in range(PEER_HEADS):
        qh = qp[:, h * PEER_KDIM:(h + 1) * PEER_KDIM]
        a = lax.dot_general(k0_ref[h], qh, _NT, preferred_element_type=F32)
        b = lax.dot_general(k1_ref[h], qh, _NT, preferred_element_type=F32)
        ra = _top_rows(a, PEER_TOPK + 1)
        rb = _top_rows(b, PEER_TOPK + 1)
        sa = jnp.concatenate(ra[:PEER_TOPK], axis=0)
        sb = jnp.concatenate(rb[:PEER_TOPK], axis=0)
        sa_tail = jnp.where(rank < n_lead, -jnp.inf, sa)
        cand = jnp.concatenate([ra[i] + sb for i in range(n_lead)] + [rb[i] + sa_tail for i in range(n_lead)], axis=0)
        top = _top_rows(cand, PEER_TOPK + 1)
        v16 = top[PEER_TOPK - 1]
        v17 = jnp.maximum(top[PEER_TOPK], jnp.maximum(ra[0] + rb[PEER_TOPK], rb[0] + ra[PEER_TOPK]))
        cut = jnp.where(v17 == -jnp.inf, v16, 0.5 * (v16 + v17))
        vmax = ra[0] + rb[0]
        z = jnp.sum(jnp.where(cand >= cut, jnp.exp(cand - vmax), 0.0), axis=0, keepdims=True)
        d_ref[h] = cut - a
        ea_ref[h] = jnp.exp(a - ra[0]) / z
        b_ref[h] = b
        eb_ref[h] = jnp.exp(b - rb[0])


def _peer_scores(xn, wq, k0, k1, tb):
    t = xn.shape[0]
    tb = min(tb, t)
    tab = pl.BlockSpec((PEER_HEADS, N_KEYS, tb), lambda i: (0, 0, i))
    tab_shape = jax.ShapeDtypeStruct((PEER_HEADS, N_KEYS, t), F32)
    full3 = lambda a: pl.BlockSpec(a.shape, lambda i: (0, 0, 0))
    return pl.pallas_call(
        _peer_score_kernel,
        grid=(t // tb,),
        in_specs=[pl.BlockSpec((tb, D_MODEL), lambda i: (i, 0)),
                  pl.BlockSpec(wq.shape, lambda i: (0, 0)), full3(k0), full3(k1)],
        out_specs=[tab, tab, tab, tab],
        out_shape=[tab_shape, tab_shape, tab_shape, tab_shape],
        compiler_params=_params("parallel"),
        name="peer_scores",
    )(xn, wq, k0, k1)


PEER_JQ = 32
MXU_N = 256


def _peer_dense_kernel(xn_ref, u_ref, vt_ref, d_ref, ea_ref, b_ref, eb_ref, o_ref, at0_ref, at1_ref, w0_ref, w1_ref, acc_ref,
                       *, tb, eb):
    s = pl.program_id(1)

    @pl.when(s == 0)
    def _():
        at1_ref[...] = jnp.zeros((eb, tb), F32)
        w0_ref[...] = jnp.zeros((eb, tb), BF16)
        acc_ref[...] = jnp.zeros_like(acc_ref)

    def stages(cur):
        at_cur, at_prev = (at0_ref, at1_ref) if cur == 0 else (at1_ref, at0_ref)
        w_cur, w_prev = (w0_ref, w1_ref) if cur == 0 else (w1_ref, w0_ref)
        n_i = eb // N_KEYS
        n_tl = tb // LANES
        mxu_n = min(MXU_N, tb)
        mxu_pieces = []
        for nt in range(tb // mxu_n):
            ns = slice(nt * mxu_n, (nt + 1) * mxu_n)
            mxu_pieces += [("pre", ns), ("acc", ns)]
        for tl in range(n_tl):
            ls = slice(tl * LANES, (tl + 1) * LANES)
            for k, (kind, ns) in enumerate(mxu_pieces):
                if k * n_tl // len(mxu_pieces) != tl:
                    continue
                if kind == "pre":
                    at_cur[:, ns] = lax.dot_general(u_ref[...], xn_ref[ns, :], _NT, preferred_element_type=F32)
                else:
                    acc_ref[:, ns] += jnp.dot(vt_ref[...], w_cur[:, ns], preferred_element_type=F32)
            for jq in range(N_KEYS // PEER_JQ):
                js = slice(jq * PEER_JQ, (jq + 1) * PEER_JQ)
                gate = [jnp.zeros((PEER_JQ, LANES), F32) for _ in range(n_i)]
                for h in range(PEER_HEADS):
                    bq = b_ref[h, js, ls]
                    ebq = eb_ref[h, js, ls]
                    for ii in range(n_i):
                        hit = bq >= d_ref[h, ii:ii + 1, ls]
                        gate[ii] = gate[ii] + jnp.where(hit, ebq * ea_ref[h, ii:ii + 1, ls], 0.0)
                for ii in range(n_i):
                    rs = slice(ii * N_KEYS + jq * PEER_JQ, ii * N_KEYS + (jq + 1) * PEER_JQ)
                    at = at_prev[rs, ls]
                    gelu = 0.5 * at * (1.0 + lax.erf(at * math.sqrt(0.5)))
                    w_prev[rs, ls] = (gate[ii] * gelu).astype(BF16)

    for parity in range(2):
        pl.when(s % 2 == parity)(functools.partial(stages, parity))

    @pl.when(s == pl.num_programs(1) - 1)
    def _():
        o_ref[...] = acc_ref[...].T


def _peer_dense(xn, u, vt, d, ea, b, eb_tab, tb, eb):
    t = xn.shape[0]
    tb = min(tb, t)
    n_e = u.shape[0] // eb
    rows = eb // N_KEYS
    clip = lambda e: jnp.clip(e, 0, n_e - 1)
    col_tab = pl.BlockSpec((PEER_HEADS, N_KEYS, tb), lambda i, s: (0, 0, i))
    row_tab = pl.BlockSpec((PEER_HEADS, rows, tb), lambda i, s: (0, clip(s - 1), i))
    return pl.pallas_call(
        functools.partial(_peer_dense_kernel, tb=tb, eb=eb),
        grid=(t // tb, n_e + 2),
        in_specs=[
            pl.BlockSpec((tb, D_MODEL), lambda i, s: (i, 0)),
            pl.BlockSpec((eb, D_MODEL), lambda i, s: (clip(s), 0)),
            pl.BlockSpec((D_MODEL, eb), lambda i, s: (0, clip(s - 2))),
            row_tab, row_tab, col_tab, col_tab,
        ],
        out_specs=pl.BlockSpec((tb, D_MODEL), lambda i, s: (i, 0)),
        out_shape=jax.ShapeDtypeStruct((t, D_MODEL), F32),
        scratch_shapes=[
            pltpu.VMEM((eb, tb), F32),
            pltpu.VMEM((eb, tb), F32),
            pltpu.VMEM((eb, tb), BF16),
            pltpu.VMEM((eb, tb), BF16),
            pltpu.VMEM((D_MODEL, tb), F32),
        ],
        compiler_params=_params("parallel", "arbitrary"),
        name="peer_dense",
    )(xn, u, vt, d, ea, b, eb_tab)


def _transpose_cast_kernel(x_ref, o_ref):
    o_ref[...] = x_ref[...].T.astype(BF16)


def _transpose_cast(x, tr, tc):
    r, c = x.shape
    return pl.pallas_call(
        _transpose_cast_kernel,
        grid=(r // tr, c // tc),
        in_specs=[pl.BlockSpec((tr, tc), lambda i, j: (i, j))],
        out_specs=pl.BlockSpec((tc, tr), lambda i, j: (j, i)),
        out_shape=jax.ShapeDtypeStruct((c, r), BF16),
        compiler_params=_params("parallel", "parallel"),
        name="transpose_cast",
    )(x)


def _rope_swap(w):
    half = w.shape[-1] // 2
    return jnp.concatenate([-w[..., half:], w[..., :half]], axis=-1)


def _prepare_weights(w_in, w_uq, w_ukv, peer_keys):
    c_rg = 2 * RET_QK_W + 2 * RET_V_W
    c_ckv = c_rg + Q_LORA
    c_kr = c_ckv + KV_LORA
    c_ga = c_kr + QK_ROPE
    w_big = jnp.concatenate([w_in[:, :c_rg], w_in[:, c_ga:]], axis=1).astype(BF16)
    w_kr = w_in[:, c_kr:c_ga]
    w_small = jnp.concatenate([w_in[:, c_rg:c_kr], w_kr, _rope_swap(w_kr)], axis=1).astype(BF16)
    q_rope = w_uq[:, :, QK_NOPE:]
    wq = jnp.concatenate([w_uq[:, :, :QK_NOPE], q_rope, _rope_swap(q_rope)], axis=-1)
    wq = wq.reshape(Q_LORA, MLA_HEADS * QK_PAD).astype(BF16)
    wk = w_ukv[:, :, :QK_NOPE].reshape(KV_LORA, MLA_HEADS * QK_NOPE).astype(BF16)
    wv = w_ukv[:, :, QK_NOPE:].reshape(KV_LORA, MLA_HEADS * V_HEAD).astype(BF16)
    w_uk_t = jnp.transpose(w_ukv[:, :, :QK_NOPE], (1, 2, 0)).astype(BF16)
    w_uv = jnp.transpose(w_ukv[:, :, QK_NOPE:], (1, 0, 2)).astype(BF16)
    zeros = jnp.zeros((PEER_HEADS, N_KEYS, PEER_HALF), F32)
    k0 = jnp.concatenate([peer_keys[:, 0], zeros], axis=-1).astype(BF16)
    k1 = jnp.concatenate([zeros, peer_keys[:, 1]], axis=-1).astype(BF16)
    return w_big, w_small, wq, wk, wv, w_uk_t, w_uv, k0, k1


def _q_rope_tables(pos):
    cos, sin = _rope_tables(pos, QK_ROPE // 2)
    zeros = jnp.zeros((pos.shape[0], LANES - QK_ROPE), F32)
    return jnp.concatenate([cos, cos, zeros], axis=1), jnp.concatenate([sin, sin, zeros], axis=1)


def kernel(x_prompt, x_sample, cache_ckv, cache_krope, state_ret, page_table, p_prompt, p_sample, norm1_g, w_in, q_norm_g, w_uq, kv_norm_g, w_ukv, w_ret_o, w_mla_o, w_o, norm2_g, peer_wq, peer_keys, peer_u, peer_v, norm3_g, w_ple_gate, w_ple, final_norm_g):
    b_p, l_p, _ = x_prompt.shape
    b_s, l_s, _ = x_sample.shape
    depth = w_in.shape[0]
    assert depth == 1
    past_len = page_table.shape[1] * PAGE_SIZE
    t_p = b_p * l_p
    t_s = b_s * l_s

    w_big, w_small, wq, wk, wv, w_uk_t, w_uv, k0, k1 = _prepare_weights(w_in[0], w_uq[0], w_ukv[0], peer_keys[0])
    w_ret_o_b = w_ret_o[0].astype(BF16)
    w_mla_o_b = w_mla_o[0].astype(BF16)
    w_o_b = w_o[0].astype(BF16)
    peer_wq_b = peer_wq[0].astype(BF16)
    u_b = peer_u[0].astype(BF16)
    vt_b = _transpose_cast(peer_v[0], 1024, 512)
    w_gate_b = w_ple_gate[0].astype(BF16)
    w_ple_b = w_ple[0].astype(BF16)

    xp = x_prompt.reshape(t_p, D_MODEL)
    xs = x_sample.reshape(t_s, D_MODEL)
    pos_p = jnp.arange(l_p, dtype=jnp.int32)
    pos_s = past_len + jnp.arange(l_s, dtype=jnp.int32)

    big_p = _norm_matmul(xp, norm1_g[0], w_big, BF16, tm=512, tn=2048)
    big_s = _norm_matmul(xs, norm1_g[0], w_big, BF16, tm=512, tn=2048)
    small_p = _norm_matmul(xp, norm1_g[0], w_small, F32, tm=512, tn=SMALL_W)
    small_s = _norm_matmul(xs, norm1_g[0], w_small, F32, tm=512, tn=SMALL_W)

    ro_p, state_p = _retention_prompt(big_p, b_p, l_p, rb=256)
    ro_s, state_s = _retention_sample(big_s, state_ret[0], b_s, l_s, past_len)

    cos_p, sin_p = _q_rope_tables(pos_p)
    cos_p = jnp.tile(cos_p, (b_p, 1))
    sin_p = jnp.tile(sin_p, (b_p, 1))
    q_p, ckv_p, kr_p, k_p, v_p = _mla_proj(small_p, q_norm_g[0], kv_norm_g[0], wq, wk, wv, cos_p, sin_p, 256, True)
    mo_p = _flash_attention(q_p, k_p, v_p, b_p, l_p, tq=512)

    cos_s, sin_s = _q_rope_tables(pos_s)
    cos_s = jnp.tile(cos_s, (b_s, 1))
    sin_s = jnp.tile(sin_s, (b_s, 1))
    q_s, ckv_s, kr_s = _mla_proj(small_s, q_norm_g[0], kv_norm_g[0], wq, wk, wv, cos_s, sin_s, 256, False)
    q_lat = _head_matmul(q_s, w_uk_t, 2, BF16).reshape(b_s, l_s * MLA_HEADS, KV_LORA)
    q_rope = q_s.reshape(t_s, MLA_HEADS, QK_PAD)[:, :, QK_NOPE:QK_NOPE + QK_ROPE].reshape(b_s, l_s * MLA_HEADS, QK_ROPE)
    o_lat = _decode_attention(q_lat, q_rope, cache_ckv, cache_krope, page_table,
                              ckv_s.reshape(b_s, l_s, KV_LORA), kr_s.reshape(b_s, l_s, QK_ROPE), pg=16)
    mo_s = _head_matmul(o_lat.reshape(t_s, MLA_HEADS * KV_LORA), w_uv, 1, BF16)

    outs = []
    for x, big, ro, mo, p_emb in ((xp, big_p, ro_p, mo_p, p_prompt[0]), (xs, big_s, ro_s, mo_s, p_sample[0])):
        t = x.shape[0]
        merged = _merge(ro, mo, w_ret_o_b, w_mla_o_b, big, tm=512, tn=512)
        h, xn = _oproj(x, merged, w_o_b, norm2_g[0], tm=256)
        d, ea, b, eb_tab = _peer_scores(xn, peer_wq_b, k0, k1, tb=256)
        y = _peer_dense(xn, u_b, vt_b, d, ea, b, eb_tab, tb=512, eb=1024)
        outs.append(_ple(h, y, p_emb.reshape(t, PLE_DIM), norm3_g[0], w_gate_b, w_ple_b, final_norm_g, tm=256))

    y_prompt = outs[0].reshape(b_p, l_p, D_MODEL)
    y_sample = outs[1].reshape(b_s, l_s, D_MODEL)
    return (y_prompt, y_sample,
            ckv_p.reshape(1, b_p, l_p, KV_LORA), kr_p.reshape(1, b_p, l_p, QK_ROPE), state_p[None],
            ckv_s.reshape(1, b_s, l_s, KV_LORA), kr_s.reshape(1, b_s, l_s, QK_ROPE), state_s[None])
```

```python
import functools
import math

import jax
import jax.numpy as jnp
from jax import lax
from jax.experimental import pallas as pl
from jax.experimental.pallas import tpu as pltpu

F32 = jnp.float32
BF16 = jnp.bfloat16

D_MODEL = 2048
RET_HEADS = 8
RET_DK = 128
RET_DV = 256
RET_CHUNK = 128
RET_QK_W = RET_HEADS * RET_DK
RET_V_W = RET_HEADS * RET_DV
MLA_HEADS = 16
Q_LORA = 768
KV_LORA = 512
QK_NOPE = 128
QK_ROPE = 64
V_HEAD = 128
MLA_SCALE = (QK_NOPE + QK_ROPE) ** -0.5
ROPE_THETA = 10000.0
PEER_HEADS = 8
N_KEYS = 128
PEER_KDIM = 128
PEER_HALF = PEER_KDIM // 2
PEER_TOPK = 16
PLE_DIM = 256
EPS = 1e-6
PAGE_SIZE = 128

LANES = 128
QK_PAD = 2 * LANES
SMALL_W = Q_LORA + KV_LORA + LANES
BIG_W = 2 * RET_QK_W + 2 * RET_V_W + 2 * D_MODEL
VMEM_LIMIT = 56 * 1024 * 1024
NEG_BIG = -1e30

_NT = (((1,), (1,)), ((), ()))
_TN = (((0,), (0,)), ((), ()))
_NN = (((1,), (0,)), ((), ()))


def _params(*sem):
    return pltpu.CompilerParams(dimension_semantics=sem, vmem_limit_bytes=VMEM_LIMIT)


def _rms_scale(x):
    return lax.rsqrt(jnp.mean(x * x, axis=-1, keepdims=True) + EPS)


def _half_swap(x):
    return pltpu.roll(x, LANES // 2, 1)


def _norm_matmul_kernel(x_ref, g_ref, w_ref, o_ref, xn_ref):
    @pl.when(pl.program_id(1) == 0)
    def _():
        x = x_ref[...]
        xn_ref[...] = (x * _rms_scale(x) * g_ref[...]).astype(BF16)

    o_ref[...] = jnp.dot(xn_ref[...], w_ref[...], preferred_element_type=F32).astype(o_ref.dtype)


def _norm_matmul(x, g, w, out_dtype, tm, tn):
    t, d = x.shape
    n = w.shape[1]
    tm = min(tm, t)
    return pl.pallas_call(
        _norm_matmul_kernel,
        grid=(t // tm, n // tn),
        in_specs=[
            pl.BlockSpec((tm, d), lambda i, j: (i, 0)),
            pl.BlockSpec((1, d), lambda i, j: (0, 0)),
            pl.BlockSpec((d, tn), lambda i, j: (0, j)),
        ],
        out_specs=pl.BlockSpec((tm, tn), lambda i, j: (i, j)),
        out_shape=jax.ShapeDtypeStruct((t, n), out_dtype),
        scratch_shapes=[pltpu.VMEM((tm, d), BF16)],
        compiler_params=_params("parallel", "arbitrary"),
        name="norm_matmul",
    )(x, g.reshape(1, d), w)


def _retention_consts(c):
    lg = jnp.log(1.0 - 2.0 ** (-5.0 - jnp.arange(RET_HEADS, dtype=F32)))
    idx = jnp.arange(c, dtype=F32)
    diff = idx[:, None] - idx[None, :]
    causal = diff >= 0
    dmat = jnp.where(causal[None], jnp.exp(lg[:, None, None] * jnp.where(causal, diff, 0.0)[None]), 0.0)
    q_dec = jnp.exp(lg[:, None] * (idx[None, :] + 1.0))
    k_dec = jnp.exp(lg[:, None] * (c - 1.0 - idx[None, :]))
    s_dec = jnp.exp(lg * c)
    qd = jnp.broadcast_to(q_dec[:, :, None], (RET_HEADS, c, RET_DK))
    kd = jnp.broadcast_to(k_dec[:, :, None], (RET_HEADS, c, RET_DK))
    sd = jnp.broadcast_to(s_dec[:, None, None], (RET_HEADS, 8, RET_DV))
    return dmat, qd, kd, sd


def _rope_tables(pos, half):
    inv = ROPE_THETA ** (-jnp.arange(half, dtype=F32) / half)
    ang = pos.astype(F32)[:, None] * inv[None, :]
    return jnp.cos(ang), jnp.sin(ang)


def _ret_head(q, k, v, s, dmat, qd, kd, sd):
    qb = q.astype(BF16)
    kb = k.astype(BF16)
    att = lax.dot_general(qb, kb, _NT, preferred_element_type=F32) * dmat
    o = jnp.dot(att.astype(BF16), v, preferred_element_type=F32)
    o = o + jnp.dot((q * qd).astype(BF16), s.astype(BF16), preferred_element_type=F32)
    s_new = s * sd + lax.dot_general((k * kd).astype(BF16), v, _TN, preferred_element_type=F32)
    return o, s_new


def _ret_prompt_kernel(rq_ref, rk_ref, rv_ref, rg_ref, cos_ref, sin_ref, dmat_ref, qd_ref, kd_ref, sd_ref,
                       o_ref, s_ref, *, n_sub):
    @pl.when(pl.program_id(1) == 0)
    def _():
        s_ref[...] = jnp.zeros_like(s_ref)

    c = RET_CHUNK
    for sub in range(n_sub):
        rows = slice(sub * c, (sub + 1) * c)
        cos = cos_ref[rows, :]
        sin = sin_ref[rows, :]
        for h in range(RET_HEADS):
            qk = slice(h * RET_DK, (h + 1) * RET_DK)
            vv = slice(h * RET_DV, (h + 1) * RET_DV)
            q = rq_ref[rows, qk].astype(F32)
            k = rk_ref[rows, qk].astype(F32)
            q = q * cos + _half_swap(q) * sin
            k = (k * cos + _half_swap(k) * sin) * (RET_DK ** -0.5)
            o, s_new = _ret_head(q, k, rv_ref[rows, vv], s_ref[0, h], dmat_ref[h], qd_ref[h], kd_ref[h],
                                 sd_ref[h, 0:1, :])
            s_ref[0, h] = s_new
            rg = rg_ref[rows, vv].astype(F32)
            o_ref[rows, vv] = (o * _rms_scale(o) * (rg * jax.nn.sigmoid(rg))).astype(BF16)


def _retention_prompt(proj, b, l, rb):
    rb = min(rb, l)
    n_r = l // rb
    cos, sin = _rope_tables(jnp.arange(l, dtype=jnp.int32), RET_DK // 2)
    cos2 = jnp.concatenate([cos, cos], axis=1)
    sin2 = jnp.concatenate([-sin, sin], axis=1)
    dmat, qd, kd, sd = _retention_consts(RET_CHUNK)
    row = lambda bi, r: bi * n_r + r
    const3 = lambda bi, r: (0, 0, 0)
    return pl.pallas_call(
        functools.partial(_ret_prompt_kernel, n_sub=rb // RET_CHUNK),
        grid=(b, n_r),
        in_specs=[
            pl.BlockSpec((rb, RET_QK_W), lambda bi, r: (row(bi, r), 0)),
            pl.BlockSpec((rb, RET_QK_W), lambda bi, r: (row(bi, r), 1)),
            pl.BlockSpec((rb, RET_V_W), lambda bi, r: (row(bi, r), 1)),
            pl.BlockSpec((rb, RET_V_W), lambda bi, r: (row(bi, r), 2)),
            pl.BlockSpec((rb, RET_DK), lambda bi, r: (r, 0)),
            pl.BlockSpec((rb, RET_DK), lambda bi, r: (r, 0)),
            pl.BlockSpec(dmat.shape, const3),
            pl.BlockSpec(qd.shape, const3),
            pl.BlockSpec(kd.shape, const3),
            pl.BlockSpec(sd.shape, const3),
        ],
        out_specs=[
            pl.BlockSpec((rb, RET_V_W), lambda bi, r: (row(bi, r), 0)),
            pl.BlockSpec((1, RET_HEADS, RET_DK, RET_DV), lambda bi, r: (bi, 0, 0, 0)),
        ],
        out_shape=[
            jax.ShapeDtypeStruct((b * l, RET_V_W), BF16),
            jax.ShapeDtypeStruct((b, RET_HEADS, RET_DK, RET_DV), F32),
        ],
        compiler_params=_params("parallel", "arbitrary"),
        name="retention_prompt",
    )(proj, proj, proj, proj, cos2, sin2, dmat, qd, kd, sd)


def _ret_sample_kernel(rq_ref, rk_ref, rv_ref, rg_ref, cos_ref, sin_ref, dmat_ref, qd_ref, kd_ref, sd_ref, s0_ref,
                       o_ref, s_ref, *, l):
    cos = cos_ref[...]
    sin = sin_ref[...]
    first = lax.broadcasted_iota(jnp.int32, (2 * l, 1), 0) < l
    for h in range(RET_HEADS):
        qk = slice(h * RET_DK, (h + 1) * RET_DK)
        vv = slice(h * RET_DV, (h + 1) * RET_DV)
        q = rq_ref[:, qk].astype(F32)
        k = rk_ref[:, qk].astype(F32)
        q = q * cos + _half_swap(q) * sin
        k = (k * cos + _half_swap(k) * sin) * (RET_DK ** -0.5)
        v = rv_ref[:, vv]
        att = lax.dot_general(q.astype(BF16), k.astype(BF16), _NT, preferred_element_type=F32) * dmat_ref[h]
        o = jnp.dot(att.astype(BF16), v, preferred_element_type=F32)
        qdb = (q * qd_ref[h]).astype(BF16)
        kdk = k * kd_ref[h]
        sd = sd_ref[h, 0:1, :]
        s_a = s0_ref[0, h]
        s_b = s0_ref[1, h]
        o = o + jnp.where(first,
                          jnp.dot(qdb, s_a.astype(BF16), preferred_element_type=F32),
                          jnp.dot(qdb, s_b.astype(BF16), preferred_element_type=F32))
        k_a = jnp.where(first, kdk, 0.0).astype(BF16)
        k_b = jnp.where(first, 0.0, kdk).astype(BF16)
        s_ref[0, h] = s_a * sd + lax.dot_general(k_a, v, _TN, preferred_element_type=F32)
        s_ref[1, h] = s_b * sd + lax.dot_general(k_b, v, _TN, preferred_element_type=F32)
        rg = rg_ref[:, vv].astype(F32)
        o_ref[:, vv] = (o * _rms_scale(o) * (rg * jax.nn.sigmoid(rg))).astype(BF16)


def _retention_sample(proj, state, b, l, pos0):
    assert b % 2 == 0
    cos, sin = _rope_tables(pos0 + jnp.arange(l, dtype=jnp.int32), RET_DK // 2)
    cos2 = jnp.tile(jnp.concatenate([cos, cos], axis=1), (2, 1))
    sin2 = jnp.tile(jnp.concatenate([-sin, sin], axis=1), (2, 1))
    dmat, qd, kd, sd = _retention_consts(l)
    zeros = jnp.zeros_like(dmat)
    dmat2 = jnp.concatenate([jnp.concatenate([dmat, zeros], axis=2), jnp.concatenate([zeros, dmat], axis=2)], axis=1)
    qd2 = jnp.tile(qd, (1, 2, 1))
    kd2 = jnp.tile(kd, (1, 2, 1))
    rows = 2 * l
    const2 = lambda i: (0, 0)
    const3 = lambda i: (0, 0, 0)
    return pl.pallas_call(
        functools.partial(_ret_sample_kernel, l=l),
        grid=(b // 2,),
        in_specs=[
            pl.BlockSpec((rows, RET_QK_W), lambda i: (i, 0)),
            pl.BlockSpec((rows, RET_QK_W), lambda i: (i, 1)),
            pl.BlockSpec((rows, RET_V_W), lambda i: (i, 1)),
            pl.BlockSpec((rows, RET_V_W), lambda i: (i, 2)),
            pl.BlockSpec((rows, RET_DK), const2),
            pl.BlockSpec((rows, RET_DK), const2),
            pl.BlockSpec(dmat2.shape, const3),
            pl.BlockSpec(qd2.shape, const3),
            pl.BlockSpec(kd2.shape, const3),
            pl.BlockSpec(sd.shape, const3),
            pl.BlockSpec((2, RET_HEADS, RET_DK, RET_DV), lambda i: (i, 0, 0, 0)),
        ],
        out_specs=[
            pl.BlockSpec((rows, RET_V_W), lambda i: (i, 0)),
            pl.BlockSpec((2, RET_HEADS, RET_DK, RET_DV), lambda i: (i, 0, 0, 0)),
        ],
        out_shape=[
            jax.ShapeDtypeStruct((b * l, RET_V_W), BF16),
            jax.ShapeDtypeStruct((b, RET_HEADS, RET_DK, RET_DV), F32),
        ],
        compiler_params=_params("parallel"),
        name="retention_sample",
    )(proj, proj, proj, proj, cos2, sin2, dmat2, qd2, kd2, sd, state)


def _mla_proj_kernel(x_ref, qg_ref, kvg_ref, wq_ref, wk_ref, wv_ref, cos_ref, sin_ref,
                     q_ref, ckv_ref, kr_ref, *kv_refs):
    cos = cos_ref[...]
    sin = sin_ref[...]
    cq = x_ref[:, 0:Q_LORA]
    cqn = (cq * _rms_scale(cq) * qg_ref[...]).astype(BF16)
    q = jnp.dot(cqn, wq_ref[...], preferred_element_type=F32)
    for h in range(MLA_HEADS):
        lo = h * QK_PAD
        q_ref[:, lo:lo + LANES] = q[:, lo:lo + LANES].astype(BF16)
        r = q[:, lo + LANES:lo + QK_PAD]
        q_ref[:, lo + LANES:lo + QK_PAD] = (r * cos + _half_swap(r) * sin).astype(BF16)
    ckv = x_ref[:, Q_LORA:Q_LORA + KV_LORA]
    ckvn = ckv * _rms_scale(ckv) * kvg_ref[...]
    ckv_ref[...] = ckvn
    krx = x_ref[:, Q_LORA + KV_LORA:SMALL_W]
    krr = krx * cos + _half_swap(krx) * sin
    kr_ref[...] = krr[:, 0:QK_ROPE]
    if kv_refs:
        k_ref, v_ref = kv_refs
        cb = ckvn.astype(BF16)
        kn = jnp.dot(cb, wk_ref[...], preferred_element_type=F32)
        krb = krr.astype(BF16)
        for h in range(MLA_HEADS):
            lo = h * QK_PAD
            k_ref[:, lo:lo + LANES] = kn[:, h * QK_NOPE:(h + 1) * QK_NOPE].astype(BF16)
            k_ref[:, lo + LANES:lo + QK_PAD] = krb
        v_ref[...] = jnp.dot(cb, wv_ref[...], preferred_element_type=F32).astype(BF16)


def _mla_proj(small, q_norm_g, kv_norm_g, wq, wk, wv, cosq, sinq, tm, with_kv):
    t = small.shape[0]
    tm = min(tm, t)
    rowb = lambda n: pl.BlockSpec((tm, n), lambda i: (i, 0))
    full = lambda a: pl.BlockSpec(a.shape, lambda i: (0, 0))
    qg = q_norm_g.reshape(1, Q_LORA)
    kvg = kv_norm_g.reshape(1, KV_LORA)
    out_specs = [rowb(MLA_HEADS * QK_PAD), rowb(KV_LORA), rowb(QK_ROPE)]
    out_shape = [
        jax.ShapeDtypeStruct((t, MLA_HEADS * QK_PAD), BF16),
        jax.ShapeDtypeStruct((t, KV_LORA), F32),
        jax.ShapeDtypeStruct((t, QK_ROPE), F32),
    ]
    if with_kv:
        out_specs += [rowb(MLA_HEADS * QK_PAD), rowb(MLA_HEADS * V_HEAD)]
        out_shape += [
            jax.ShapeDtypeStruct((t, MLA_HEADS * QK_PAD), BF16),
            jax.ShapeDtypeStruct((t, MLA_HEADS * V_HEAD), BF16),
        ]
    return pl.pallas_call(
        _mla_proj_kernel,
        grid=(t // tm,),
        in_specs=[rowb(SMALL_W), full(qg), full(kvg), full(wq), full(wk), full(wv), rowb(LANES), rowb(LANES)],
        out_specs=out_specs,
        out_shape=out_shape,
        compiler_params=_params("parallel"),
        name="mla_proj",
    )(small, qg, kvg, wq, wk, wv, cosq, sinq)


FLASH_HEADS = 2


def _flash_kernel(q_ref, k_ref, v_ref, o_ref, *, tq):
    qi = pl.program_id(2)
    qs = [q_ref[:, g * QK_PAD:(g + 1) * QK_PAD] for g in range(FLASH_HEADS)]

    def scores(j, g):
        off = pl.multiple_of(j * tq, tq)
        k = k_ref[pl.ds(off, tq), g * QK_PAD:(g + 1) * QK_PAD]
        v = v_ref[pl.ds(off, tq), g * V_HEAD:(g + 1) * V_HEAD]
        s = lax.dot_general(qs[g], k, _NT, preferred_element_type=F32) * MLA_SCALE
        return s, v

    def update(carry, s, v):
        m, l, acc = carry
        m_new = jnp.maximum(m, jnp.max(s, axis=-1, keepdims=True))
        corr = jnp.exp(m - m_new)
        p = jnp.exp(s - m_new)
        l = l * corr + jnp.sum(p, axis=-1, keepdims=True)
        acc = acc * corr + jnp.dot(p.astype(BF16), v, preferred_element_type=F32)
        return m_new, l, acc

    def body(j, carries):
        return tuple(update(carries[g], *scores(j, g)) for g in range(FLASH_HEADS))

    init = (jnp.full((tq, 1), NEG_BIG, F32), jnp.zeros((tq, 1), F32), jnp.zeros((tq, V_HEAD), F32))
    carries = lax.fori_loop(0, qi, body, (init,) * FLASH_HEADS)
    row = lax.broadcasted_iota(jnp.int32, (tq, tq), 0)
    col = lax.broadcasted_iota(jnp.int32, (tq, tq), 1)
    for g in range(FLASH_HEADS):
        s, v = scores(qi, g)
        _, l, acc = update(carries[g], jnp.where(col <= row, s, -jnp.inf), v)
        o_ref[:, g * V_HEAD:(g + 1) * V_HEAD] = (acc / l).astype(BF16)


def _flash_attention(q, k, v, b, l, tq):
    tq = min(tq, l)
    n_q = l // tq
    qw = FLASH_HEADS * QK_PAD
    vw = FLASH_HEADS * V_HEAD
    return pl.pallas_call(
        functools.partial(_flash_kernel, tq=tq),
        grid=(b, MLA_HEADS // FLASH_HEADS, n_q),
        in_specs=[
            pl.BlockSpec((tq, qw), lambda bi, h, i: (bi * n_q + i, h)),
            pl.BlockSpec((l, qw), lambda bi, h, i: (bi, h)),
            pl.BlockSpec((l, vw), lambda bi, h, i: (bi, h)),
        ],
        out_specs=pl.BlockSpec((tq, vw), lambda bi, h, i: (bi * n_q + i, h)),
        out_shape=jax.ShapeDtypeStruct((b * l, MLA_HEADS * V_HEAD), BF16),
        compiler_params=_params("parallel", "parallel", "arbitrary"),
        name="mla_prompt_attention",
    )(q, k, v)


def _head_matmul_kernel(x_ref, w_ref, o_ref):
    o_ref[...] = jnp.dot(x_ref[...], w_ref[0], preferred_element_type=F32).astype(o_ref.dtype)


def _head_matmul(x, w, col_stride, out_dtype):
    m = x.shape[0]
    heads, kd, n = w.shape
    return pl.pallas_call(
        _head_matmul_kernel,
        grid=(heads,),
        in_specs=[
            pl.BlockSpec((m, kd), lambda h: (0, h * col_stride)),
            pl.BlockSpec((1, kd, n), lambda h: (h, 0, 0)),
        ],
        out_specs=pl.BlockSpec((m, n), lambda h: (0, h)),
        out_shape=jax.ShapeDtypeStruct((m, heads * n), out_dtype),
        compiler_params=_params("parallel"),
        name="head_matmul",
    )(x, w)


def _decode_kernel(pt_ref, ql_ref, qr_ref, ckv_hbm, krt_hbm, cn_ref, kn_ref, o_ref,
                   m_ref, l_ref, acc_ref, kc_ref, krc_ref, kbuf, krbuf, sem, *, pg, l_new):
    b = pl.program_id(0)
    g = pl.program_id(1)
    n_b = pl.num_programs(0)
    n_g = pl.num_programs(1)
    step = b * n_g + g
    slot = step % 2

    def page_copies(bi, gi, sl):
        copies = []
        for i in range(pg):
            page = pt_ref[bi, gi * pg + i]
            rs = pl.ds(i * PAGE_SIZE, PAGE_SIZE)
            copies.append(pltpu.make_async_copy(ckv_hbm.at[0, page], kbuf.at[sl, rs, :], sem.at[sl]))
            copies.append(pltpu.make_async_copy(krt_hbm.at[0, page], krbuf.at[sl, :, rs], sem.at[sl]))
        return copies

    @pl.when(step == 0)
    def _():
        for c in page_copies(b, g, slot):
            c.start()

    last_g = g == n_g - 1
    nb = jnp.where(last_g, b + 1, b)
    ng = jnp.where(last_g, 0, g + 1)

    @pl.when(step + 1 < n_b * n_g)
    def _():
        for c in page_copies(nb, ng, 1 - slot):
            c.start()

    for c in page_copies(b, g, slot):
        c.wait()

    @pl.when(g == 0)
    def _():
        m_ref[...] = jnp.full_like(m_ref, NEG_BIG)
        l_ref[...] = jnp.zeros_like(l_ref)
        acc_ref[...] = jnp.zeros_like(acc_ref)

    ql = ql_ref[0]
    qr = qr_ref[0]

    def absorb(s, vals):
        m = m_ref[...]
        m_new = jnp.maximum(m, jnp.max(s, axis=-1, keepdims=True))
        corr = jnp.exp(m - m_new)
        p = jnp.exp(s - m_new[:, 0:1])
        l_ref[...] = l_ref[...] * corr + jnp.sum(p, axis=-1, keepdims=True)
        pv = acc_ref[...] * corr[:, 0:1]
        for i, val in enumerate(vals):
            w = val.shape[0]
            pv = pv + jnp.dot(p[:, i * w:(i + 1) * w].astype(BF16), val, preferred_element_type=F32)
        acc_ref[...] = pv
        m_ref[...] = m_new

    def score(ckv, kr, kr_dims):
        return (lax.dot_general(ql, ckv, _NT, preferred_element_type=F32)
                + lax.dot_general(qr, kr, kr_dims, preferred_element_type=F32)) * MLA_SCALE

    kc_ref[...] = kbuf[slot].astype(BF16)
    krc_ref[...] = krbuf[slot].astype(BF16)
    kc = kc_ref[...]
    absorb(score(kc, krc_ref[...], _NN), [kc])

    @pl.when(g == pl.num_programs(1) - 1)
    def _():
        cn = cn_ref[0].astype(BF16)
        s_new = score(cn, kn_ref[0].astype(BF16), _NT)
        tok = lax.broadcasted_iota(jnp.int32, s_new.shape, 0) // MLA_HEADS
        col = lax.broadcasted_iota(jnp.int32, s_new.shape, 1)
        absorb(jnp.where((col <= tok) & (col < l_new), s_new, -jnp.inf), [cn])
        o_ref[0] = (acc_ref[...] / l_ref[:, 0:1]).astype(BF16)


def _decode_attention(q_lat, q_rope, cache_ckv, cache_krope_t, page_table, ckv_new, kr_new, pg):
    b, rows, _ = q_lat.shape
    n_pages = page_table.shape[1]
    pg = min(pg, n_pages)
    l_new = ckv_new.shape[1]
    pad = 16 - l_new
    cn = jnp.pad(ckv_new, ((0, 0), (0, pad), (0, 0)))
    kn = jnp.pad(kr_new, ((0, 0), (0, pad), (0, 0)))
    per_b = lambda n: pl.BlockSpec((1, n[0], n[1]), lambda bi, g, pt: (bi, 0, 0))
    hbm = pl.BlockSpec(memory_space=pl.ANY)
    keys = pg * PAGE_SIZE
    grid_spec = pltpu.PrefetchScalarGridSpec(
        num_scalar_prefetch=1,
        grid=(b, n_pages // pg),
        in_specs=[per_b((rows, KV_LORA)), per_b((rows, QK_ROPE)), hbm, hbm,
                  per_b((16, KV_LORA)), per_b((16, QK_ROPE))],
        out_specs=per_b((rows, KV_LORA)),
        scratch_shapes=[
            pltpu.VMEM((rows, LANES), F32),
            pltpu.VMEM((rows, LANES), F32),
            pltpu.VMEM((rows, KV_LORA), F32),
            pltpu.VMEM((keys, KV_LORA), BF16),
            pltpu.VMEM((QK_ROPE, keys), BF16),
            pltpu.VMEM((2, keys, KV_LORA), F32),
            pltpu.VMEM((2, QK_ROPE, keys), F32),
            pltpu.SemaphoreType.DMA((2,)),
        ],
    )
    return pl.pallas_call(
        functools.partial(_decode_kernel, pg=pg, l_new=l_new),
        grid_spec=grid_spec,
        out_shape=jax.ShapeDtypeStruct((b, rows, KV_LORA), BF16),
        compiler_params=_params("arbitrary", "arbitrary"),
        name="mla_decode_attention",
    )(page_table, q_lat, q_rope, cache_ckv, cache_krope_t, cn, kn)


def _merge_kernel(ro_ref, mo_ref, wr_ref, wm_ref, ga_ref, gb_ref, o_ref):
    t1 = jnp.dot(ro_ref[...], wr_ref[...], preferred_element_type=F32)
    t2 = jnp.dot(mo_ref[...], wm_ref[...], preferred_element_type=F32)
    ga = jax.nn.sigmoid(ga_ref[...].astype(F32))
    gb = jax.nn.sigmoid(gb_ref[...].astype(F32))
    o_ref[...] = (ga * t1 + gb * t2).astype(BF16)


def _merge(ro, mo, w_ret_o, w_mla_o, proj, tm, tn):
    t = ro.shape[0]
    tm = min(tm, t)
    ga0 = (2 * RET_QK_W + 2 * RET_V_W) // tn
    gb0 = ga0 + D_MODEL // tn
    return pl.pallas_call(
        _merge_kernel,
        grid=(t // tm, D_MODEL // tn),
        in_specs=[
            pl.BlockSpec((tm, RET_V_W), lambda i, j: (i, 0)),
            pl.BlockSpec((tm, MLA_HEADS * V_HEAD), lambda i, j: (i, 0)),
            pl.BlockSpec((RET_V_W, tn), lambda i, j: (0, j)),
            pl.BlockSpec((MLA_HEADS * V_HEAD, tn), lambda i, j: (0, j)),
            pl.BlockSpec((tm, tn), lambda i, j: (i, ga0 + j)),
            pl.BlockSpec((tm, tn), lambda i, j: (i, gb0 + j)),
        ],
        out_specs=pl.BlockSpec((tm, tn), lambda i, j: (i, j)),
        out_shape=jax.ShapeDtypeStruct((t, D_MODEL), BF16),
        compiler_params=_params("parallel", "arbitrary"),
        name="branch_merge",
    )(ro, mo, w_ret_o, w_mla_o, proj, proj)


def _oproj_kernel(x_ref, m_ref, w_ref, g_ref, h_ref, xn_ref, xnt_ref):
    h = x_ref[...] + jnp.dot(m_ref[...], w_ref[...], preferred_element_type=F32)
    h_ref[...] = h
    xn = h * _rms_scale(h) * g_ref[...]
    xn_ref[...] = xn.astype(BF16)
    xnt_ref[...] = xn.T.astype(BF16)


def _oproj(x, merged, w_o, norm2_g, tm):
    t = x.shape[0]
    tm = min(tm, t)
    rowb = pl.BlockSpec((tm, D_MODEL), lambda i: (i, 0))
    return pl.pallas_call(
        _oproj_kernel,
        grid=(t // tm,),
        in_specs=[rowb, rowb, pl.BlockSpec((D_MODEL, D_MODEL), lambda i: (0, 0)),
                  pl.BlockSpec((1, D_MODEL), lambda i: (0, 0))],
        out_specs=[rowb, rowb, pl.BlockSpec((D_MODEL, tm), lambda i: (0, i))],
        out_shape=[jax.ShapeDtypeStruct((t, D_MODEL), F32), jax.ShapeDtypeStruct((t, D_MODEL), BF16),
                   jax.ShapeDtypeStruct((D_MODEL, t), BF16)],
        compiler_params=_params("parallel"),
        name="out_proj_residual",
    )(x, merged, w_o, norm2_g.reshape(1, D_MODEL))


def _ple_kernel(h_ref, y_ref, p_ref, g3_ref, wg_ref, wp_ref, gf_ref, o_ref):
    h = h_ref[...] + y_ref[...]
    n3 = (h * _rms_scale(h) * g3_ref[...]).astype(BF16)
    gate = jax.nn.sigmoid(jnp.dot(n3, wg_ref[...], preferred_element_type=F32))
    pe = jnp.dot(p_ref[...].astype(BF16), wp_ref[...], preferred_element_type=F32)
    h = h + gate * pe
    o_ref[...] = h * _rms_scale(h) * gf_ref[...]


def _ple(h, y, p_emb, norm3_g, w_gate, w_ple, final_g, tm):
    t = h.shape[0]
    tm = min(tm, t)
    rowb = pl.BlockSpec((tm, D_MODEL), lambda i: (i, 0))
    vec = pl.BlockSpec((1, D_MODEL), lambda i: (0, 0))
    return pl.pallas_call(
        _ple_kernel,
        grid=(t // tm,),
        in_specs=[rowb, rowb, pl.BlockSpec((tm, PLE_DIM), lambda i: (i, 0)), vec,
                  pl.BlockSpec((D_MODEL, D_MODEL), lambda i: (0, 0)),
                  pl.BlockSpec((PLE_DIM, D_MODEL), lambda i: (0, 0)), vec],
        out_specs=rowb,
        out_shape=jax.ShapeDtypeStruct((t, D_MODEL), F32),
        compiler_params=_params("parallel"),
        name="ple_gate_final_norm",
    )(h, y, p_emb, norm3_g.reshape(1, D_MODEL), w_gate, w_ple, final_g.reshape(1, D_MODEL))


def _top_rows(w, n):
    rows = []
    for k in range(n):
        m = jnp.max(w, axis=0, keepdims=True)
        rows.append(m)
        if k + 1 < n:
            w = jnp.where(w == m, -jnp.inf, w)
    return rows


def _peer_score_kernel(xn_ref, wq_ref, k0_ref, k1_ref, d_ref, ea_ref, b_ref, eb_ref):
    qp = jnp.dot(xn_ref[...], wq_ref[...], preferred_element_type=F32).astype(BF16)
    rank = lax.broadcasted_iota(jnp.int32, (PEER_TOPK, 1), 0)
    n_lead = 4
    for h in range(PEER_HEADS):
        qh = qp[:, h * PEER_KDIM:(h + 1) * PEER_KDIM]
        a = lax.dot_general(k0_ref[h], qh, _NT, preferred_element_type=F32)
        b = lax.dot_general(k1_ref[h], qh, _NT, preferred_element_type=F32)
        ra = _top_rows(a, PEER_TOPK + 1)
        rb = _top_rows(b, PEER_TOPK + 1)
        sa = jnp.concatenate(ra[:PEER_TOPK], axis=0)
        sb = jnp.concatenate(rb[:PEER_TOPK], axis=0)
        sa_tail = jnp.where(rank < n_lead, -jnp.inf, sa)
        cand = jnp.concatenate([ra[i] + sb for i in range(n_lead)] + [rb[i] + sa_tail for i in range(n_lead)], axis=0)
        top = _top_rows(cand, PEER_TOPK + 1)
        v16 = top[PEER_TOPK - 1]
        v17 = jnp.maximum(top[PEER_TOPK], jnp.maximum(ra[0] + rb[PEER_TOPK], rb[0] + ra[PEER_TOPK]))
        cut = jnp.where(v17 == -jnp.inf, v16, 0.5 * (v16 + v17))
        vmax = ra[0] + rb[0]
        z = jnp.sum(jnp.where(cand >= cut, jnp.exp(cand - vmax), 0.0), axis=0, keepdims=True)
        d_ref[h] = cut - a
        ea_ref[h] = jnp.exp(a - ra[0]) / z
        b_ref[h] = b
        eb_ref[h] = jnp.exp(b - rb[0])


def _peer_scores(xn, wq, k0, k1, tb):
    t = xn.shape[0]
    tb = min(tb, t)
    tab = pl.BlockSpec((PEER_HEADS, N_KEYS, tb), lambda i: (0, 0, i))
    tab_shape = jax.ShapeDtypeStruct((PEER_HEADS, N_KEYS, t), F32)
    full3 = lambda a: pl.BlockSpec(a.shape, lambda i: (0, 0, 0))
    return pl.pallas_call(
        _peer_score_kernel,
        grid=(t // tb,),
        in_specs=[pl.BlockSpec((tb, D_MODEL), lambda i: (i, 0)),
                  pl.BlockSpec(wq.shape, lambda i: (0, 0)), full3(k0), full3(k1)],
        out_specs=[tab, tab, tab, tab],
        out_shape=[tab_shape, tab_shape, tab_shape, tab_shape],
        compiler_params=_params("parallel"),
        name="peer_scores",
    )(xn, wq, k0, k1)


PEER_JQ = 32


def _peer_dense_kernel(xnt_ref, u_ref, vt_ref, d_ref, ea_ref, b_ref, eb_ref, o_ref, at_ref, w_ref, acc_ref, *, tb, eb):
    e = pl.program_id(1)

    @pl.when(e == 0)
    def _():
        acc_ref[...] = jnp.zeros_like(acc_ref)

    at_ref[...] = jnp.dot(u_ref[...], xnt_ref[...], preferred_element_type=F32)

    n_i = eb // N_KEYS

    def gate_rows(jq, carry):
        j0 = pl.multiple_of(jq * PEER_JQ, PEER_JQ)
        for tl in range(tb // LANES):
            ls = slice(tl * LANES, (tl + 1) * LANES)
            gate = [jnp.zeros((PEER_JQ, LANES), F32) for _ in range(n_i)]
            for h in range(PEER_HEADS):
                bq = b_ref[h, pl.ds(j0, PEER_JQ), ls]
                ebq = eb_ref[h, pl.ds(j0, PEER_JQ), ls]
                for ii in range(n_i):
                    hit = bq >= d_ref[h, ii:ii + 1, ls]
                    gate[ii] = gate[ii] + jnp.where(hit, ebq * ea_ref[h, ii:ii + 1, ls], 0.0)
            for ii in range(n_i):
                r = pl.multiple_of(ii * N_KEYS + j0, PEER_JQ)
                at = at_ref[pl.ds(r, PEER_JQ), ls]
                gelu = 0.5 * at * (1.0 + lax.erf(at * math.sqrt(0.5)))
                w_ref[pl.ds(r, PEER_JQ), ls] = (gate[ii] * gelu).astype(BF16)
        return carry

    lax.fori_loop(0, N_KEYS // PEER_JQ, gate_rows, 0)

    acc_ref[...] += jnp.dot(vt_ref[...], w_ref[...], preferred_element_type=F32)

    @pl.when(e == pl.num_programs(1) - 1)
    def _():
        o_ref[...] = acc_ref[...].T


def _peer_dense(xnt, u, vt, d, ea, b, eb_tab, tb, eb):
    t = xnt.shape[1]
    tb = min(tb, t)
    rows = eb // N_KEYS
    col_tab = pl.BlockSpec((PEER_HEADS, N_KEYS, tb), lambda i, e: (0, 0, i))
    row_tab = pl.BlockSpec((PEER_HEADS, rows, tb), lambda i, e: (0, e, i))
    return pl.pallas_call(
        functools.partial(_peer_dense_kernel, tb=tb, eb=eb),
        grid=(t // tb, u.shape[0] // eb),
        in_specs=[
            pl.BlockSpec((D_MODEL, tb), lambda i, e: (0, i)),
            pl.BlockSpec((eb, D_MODEL), lambda i, e: (e, 0)),
            pl.BlockSpec((D_MODEL, eb), lambda i, e: (0, e)),
            row_tab, row_tab, col_tab, col_tab,
        ],
        out_specs=pl.BlockSpec((tb, D_MODEL), lambda i, e: (i, 0)),
        out_shape=jax.ShapeDtypeStruct((t, D_MODEL), F32),
        scratch_shapes=[
            pltpu.VMEM((eb, tb), F32),
            pltpu.VMEM((eb, tb), BF16),
            pltpu.VMEM((D_MODEL, tb), F32),
        ],
        compiler_params=_params("parallel", "arbitrary"),
        name="peer_dense",
    )(xnt, u, vt, d, ea, b, eb_tab)


def _transpose_cast_kernel(x_ref, o_ref):
    o_ref[...] = x_ref[...].T.astype(BF16)


def _transpose_cast(x, tr, tc):
    r, c = x.shape
    return pl.pallas_call(
        _transpose_cast_kernel,
        grid=(r // tr, c // tc),
        in_specs=[pl.BlockSpec((tr, tc), lambda i, j: (i, j))],
        out_specs=pl.BlockSpec((tc, tr), lambda i, j: (j, i)),
        out_shape=jax.ShapeDtypeStruct((c, r), BF16),
        compiler_params=_params("parallel", "parallel"),
        name="transpose_cast",
    )(x)


def _rope_swap(w):
    half = w.shape[-1] // 2
    return jnp.concatenate([-w[..., half:], w[..., :half]], axis=-1)


def _prepare_weights(w_in, w_uq, w_ukv, peer_keys):
    c_rg = 2 * RET_QK_W + 2 * RET_V_W
    c_ckv = c_rg + Q_LORA
    c_kr = c_ckv + KV_LORA
    c_ga = c_kr + QK_ROPE
    w_big = jnp.concatenate([w_in[:, :c_rg], w_in[:, c_ga:]], axis=1).astype(BF16)
    w_kr = w_in[:, c_kr:c_ga]
    w_small = jnp.concatenate([w_in[:, c_rg:c_kr], w_kr, _rope_swap(w_kr)], axis=1).astype(BF16)
    q_rope = w_uq[:, :, QK_NOPE:]
    wq = jnp.concatenate([w_uq[:, :, :QK_NOPE], q_rope, _rope_swap(q_rope)], axis=-1)
    wq = wq.reshape(Q_LORA, MLA_HEADS * QK_PAD).astype(BF16)
    wk = w_ukv[:, :, :QK_NOPE].reshape(KV_LORA, MLA_HEADS * QK_NOPE).astype(BF16)
    wv = w_ukv[:, :, QK_NOPE:].reshape(KV_LORA, MLA_HEADS * V_HEAD).astype(BF16)
    w_uk_t = jnp.transpose(w_ukv[:, :, :QK_NOPE], (1, 2, 0)).astype(BF16)
    w_uv = jnp.transpose(w_ukv[:, :, QK_NOPE:], (1, 0, 2)).astype(BF16)
    zeros = jnp.zeros((PEER_HEADS, N_KEYS, PEER_HALF), F32)
    k0 = jnp.concatenate([peer_keys[:, 0], zeros], axis=-1).astype(BF16)
    k1 = jnp.concatenate([zeros, peer_keys[:, 1]], axis=-1).astype(BF16)
    return w_big, w_small, wq, wk, wv, w_uk_t, w_uv, k0, k1


def _q_rope_tables(pos):
    cos, sin = _rope_tables(pos, QK_ROPE // 2)
    zeros = jnp.zeros((pos.shape[0], LANES - QK_ROPE), F32)
    return jnp.concatenate([cos, cos, zeros], axis=1), jnp.concatenate([sin, sin, zeros], axis=1)


def kernel(x_prompt, x_sample, cache_ckv, cache_krope, state_ret, page_table, p_prompt, p_sample, norm1_g, w_in, q_norm_g, w_uq, kv_norm_g, w_ukv, w_ret_o, w_mla_o, w_o, norm2_g, peer_wq, peer_keys, peer_u, peer_v, norm3_g, w_ple_gate, w_ple, final_norm_g):
    b_p, l_p, _ = x_prompt.shape
    b_s, l_s, _ = x_sample.shape
    depth = w_in.shape[0]
    assert depth == 1
    past_len = page_table.shape[1] * PAGE_SIZE
    t_p = b_p * l_p
    t_s = b_s * l_s

    w_big, w_small, wq, wk, wv, w_uk_t, w_uv, k0, k1 = _prepare_weights(w_in[0], w_uq[0], w_ukv[0], peer_keys[0])
    w_ret_o_b = w_ret_o[0].astype(BF16)
    w_mla_o_b = w_mla_o[0].astype(BF16)
    w_o_b = w_o[0].astype(BF16)
    peer_wq_b = peer_wq[0].astype(BF16)
    u_b = peer_u[0].astype(BF16)
    vt_b = _transpose_cast(peer_v[0], 1024, 512)
    w_gate_b = w_ple_gate[0].astype(BF16)
    w_ple_b = w_ple[0].astype(BF16)

    xp = x_prompt.reshape(t_p, D_MODEL)
    xs = x_sample.reshape(t_s, D_MODEL)
    pos_p = jnp.arange(l_p, dtype=jnp.int32)
    pos_s = past_len + jnp.arange(l_s, dtype=jnp.int32)

    big_p = _norm_matmul(xp, norm1_g[0], w_big, BF16, tm=512, tn=2048)
    big_s = _norm_matmul(xs, norm1_g[0], w_big, BF16, tm=512, tn=2048)
    small_p = _norm_matmul(xp, norm1_g[0], w_small, F32, tm=512, tn=SMALL_W)
    small_s = _norm_matmul(xs, norm1_g[0], w_small, F32, tm=512, tn=SMALL_W)

    ro_p, state_p = _retention_prompt(big_p, b_p, l_p, rb=256)
    ro_s, state_s = _retention_sample(big_s, state_ret[0], b_s, l_s, past_len)

    cos_p, sin_p = _q_rope_tables(pos_p)
    cos_p = jnp.tile(cos_p, (b_p, 1))
    sin_p = jnp.tile(sin_p, (b_p, 1))
    q_p, ckv_p, kr_p, k_p, v_p = _mla_proj(small_p, q_norm_g[0], kv_norm_g[0], wq, wk, wv, cos_p, sin_p, 256, True)
    mo_p = _flash_attention(q_p, k_p, v_p, b_p, l_p, tq=512)

    cos_s, sin_s = _q_rope_tables(pos_s)
    cos_s = jnp.tile(cos_s, (b_s, 1))
    sin_s = jnp.tile(sin_s, (b_s, 1))
    q_s, ckv_s, kr_s = _mla_proj(small_s, q_norm_g[0], kv_norm_g[0], wq, wk, wv, cos_s, sin_s, 256, False)
    q_lat = _head_matmul(q_s, w_uk_t, 2, BF16).reshape(b_s, l_s * MLA_HEADS, KV_LORA)
    q_rope = q_s.reshape(t_s, MLA_HEADS, QK_PAD)[:, :, QK_NOPE:QK_NOPE + QK_ROPE].reshape(b_s, l_s * MLA_HEADS, QK_ROPE)
    o_lat = _decode_attention(q_lat, q_rope, cache_ckv, jnp.swapaxes(cache_krope, 2, 3), page_table,
                              ckv_s.reshape(b_s, l_s, KV_LORA), kr_s.reshape(b_s, l_s, QK_ROPE), pg=16)
    mo_s = _head_matmul(o_lat.reshape(t_s, MLA_HEADS * KV_LORA), w_uv, 1, BF16)

    outs = []
    for x, big, ro, mo, p_emb in ((xp, big_p, ro_p, mo_p, p_prompt[0]), (xs, big_s, ro_s, mo_s, p_sample[0])):
        t = x.shape[0]
        merged = _merge(ro, mo, w_ret_o_b, w_mla_o_b, big, tm=512, tn=512)
        h, xn, xnt = _oproj(x, merged, w_o_b, norm2_g[0], tm=256)
        d, ea, b, eb_tab = _peer_scores(xn, peer_wq_b, k0, k1, tb=256)
        y = _peer_dense(xnt, u_b, vt_b, d, ea, b, eb_tab, tb=512, eb=1024)
        outs.append(_ple(h, y, p_emb.reshape(t, PLE_DIM), norm3_g[0], w_gate_b, w_ple_b, final_norm_g, tm=256))

    y_prompt = outs[0].reshape(b_p, l_p, D_MODEL)
    y_sample = outs[1].reshape(b_s, l_s, D_MODEL)
    return (y_prompt, y_sample,
            ckv_p.reshape(1, b_p, l_p, KV_LORA), kr_p.reshape(1, b_p, l_p, QK_ROPE), state_p[None],
            ckv_s.reshape(1, b_s, l_s, KV_LORA), kr_s.reshape(1, b_s, l_s, QK_ROPE), state_s[None])
```

```python
import functools
import math

import jax
import jax.numpy as jnp
from jax import lax
from jax.experimental import pallas as pl
from jax.experimental.pallas import tpu as pltpu

F32 = jnp.float32
BF16 = jnp.bfloat16

D_MODEL = 2048
RET_HEADS = 8
RET_DK = 128
RET_DV = 256
RET_CHUNK = 128
RET_QK_W = RET_HEADS * RET_DK
RET_V_W = RET_HEADS * RET_DV
MLA_HEADS = 16
Q_LORA = 768
KV_LORA = 512
QK_NOPE = 128
QK_ROPE = 64
V_HEAD = 128
MLA_SCALE = (QK_NOPE + QK_ROPE) ** -0.5
ROPE_THETA = 10000.0
PEER_HEADS = 8
N_KEYS = 128
PEER_KDIM = 128
PEER_HALF = PEER_KDIM // 2
PEER_TOPK = 16
PLE_DIM = 256
EPS = 1e-6
PAGE_SIZE = 128

LANES = 128
QK_PAD = 2 * LANES
SMALL_W = Q_LORA + KV_LORA + LANES
BIG_W = 2 * RET_QK_W + 2 * RET_V_W + 2 * D_MODEL
VMEM_LIMIT = 56 * 1024 * 1024
NEG_BIG = -1e30

_NT = (((1,), (1,)), ((), ()))
_TN = (((0,), (0,)), ((), ()))
_NN = (((1,), (0,)), ((), ()))


def _params(*sem):
    return pltpu.CompilerParams(dimension_semantics=sem, vmem_limit_bytes=VMEM_LIMIT)


def _rms_scale(x):
    return lax.rsqrt(jnp.mean(x * x, axis=-1, keepdims=True) + EPS)


def _half_swap(x):
    return pltpu.roll(x, LANES // 2, 1)


def _norm_matmul_kernel(x_ref, g_ref, w_ref, o_ref, xn_ref):
    @pl.when(pl.program_id(1) == 0)
    def _():
        x = x_ref[...]
        xn_ref[...] = (x * _rms_scale(x) * g_ref[...]).astype(BF16)

    o_ref[...] = jnp.dot(xn_ref[...], w_ref[...], preferred_element_type=F32).astype(o_ref.dtype)


def _norm_matmul(x, g, w, out_dtype, tm, tn):
    t, d = x.shape
    n = w.shape[1]
    tm = min(tm, t)
    return pl.pallas_call(
        _norm_matmul_kernel,
        grid=(t // tm, n // tn),
        in_specs=[
            pl.BlockSpec((tm, d), lambda i, j: (i, 0)),
            pl.BlockSpec((1, d), lambda i, j: (0, 0)),
            pl.BlockSpec((d, tn), lambda i, j: (0, j)),
        ],
        out_specs=pl.BlockSpec((tm, tn), lambda i, j: (i, j)),
        out_shape=jax.ShapeDtypeStruct((t, n), out_dtype),
        scratch_shapes=[pltpu.VMEM((tm, d), BF16)],
        compiler_params=_params("parallel", "arbitrary"),
        name="norm_matmul",
    )(x, g.reshape(1, d), w)


def _retention_consts(c):
    lg = jnp.log(1.0 - 2.0 ** (-5.0 - jnp.arange(RET_HEADS, dtype=F32)))
    idx = jnp.arange(c, dtype=F32)
    diff = idx[:, None] - idx[None, :]
    causal = diff >= 0
    dmat = jnp.where(causal[None], jnp.exp(lg[:, None, None] * jnp.where(causal, diff, 0.0)[None]), 0.0)
    q_dec = jnp.exp(lg[:, None] * (idx[None, :] + 1.0))
    k_dec = jnp.exp(lg[:, None] * (c - 1.0 - idx[None, :]))
    s_dec = jnp.exp(lg * c)
    qd = jnp.broadcast_to(q_dec[:, :, None], (RET_HEADS, c, RET_DK))
    kd = jnp.broadcast_to(k_dec[:, :, None], (RET_HEADS, c, RET_DK))
    sd = jnp.broadcast_to(s_dec[:, None, None], (RET_HEADS, 8, RET_DV))
    return dmat, qd, kd, sd


def _rope_tables(pos, half):
    inv = ROPE_THETA ** (-jnp.arange(half, dtype=F32) / half)
    ang = pos.astype(F32)[:, None] * inv[None, :]
    return jnp.cos(ang), jnp.sin(ang)


def _ret_head(q, k, v, s, dmat, qd, kd, sd):
    qb = q.astype(BF16)
    kb = k.astype(BF16)
    att = lax.dot_general(qb, kb, _NT, preferred_element_type=F32) * dmat
    o = jnp.dot(att.astype(BF16), v, preferred_element_type=F32)
    o = o + jnp.dot((q * qd).astype(BF16), s.astype(BF16), preferred_element_type=F32)
    s_new = s * sd + lax.dot_general((k * kd).astype(BF16), v, _TN, preferred_element_type=F32)
    return o, s_new


def _ret_prompt_kernel(rq_ref, rk_ref, rv_ref, rg_ref, cos_ref, sin_ref, dmat_ref, qd_ref, kd_ref, sd_ref,
                       o_ref, s_ref, *, n_sub):
    @pl.when(pl.program_id(1) == 0)
    def _():
        s_ref[...] = jnp.zeros_like(s_ref)

    c = RET_CHUNK
    for sub in range(n_sub):
        rows = slice(sub * c, (sub + 1) * c)
        cos = cos_ref[rows, :]
        sin = sin_ref[rows, :]
        for h in range(RET_HEADS):
            qk = slice(h * RET_DK, (h + 1) * RET_DK)
            vv = slice(h * RET_DV, (h + 1) * RET_DV)
            q = rq_ref[rows, qk].astype(F32)
            k = rk_ref[rows, qk].astype(F32)
            q = q * cos + _half_swap(q) * sin
            k = (k * cos + _half_swap(k) * sin) * (RET_DK ** -0.5)
            o, s_new = _ret_head(q, k, rv_ref[rows, vv], s_ref[0, h], dmat_ref[h], qd_ref[h], kd_ref[h],
                                 sd_ref[h, 0:1, :])
            s_ref[0, h] = s_new
            rg = rg_ref[rows, vv].astype(F32)
            o_ref[rows, vv] = (o * _rms_scale(o) * (rg * jax.nn.sigmoid(rg))).astype(BF16)


def _retention_prompt(proj, b, l, rb):
    rb = min(rb, l)
    n_r = l // rb
    cos, sin = _rope_tables(jnp.arange(l, dtype=jnp.int32), RET_DK // 2)
    cos2 = jnp.concatenate([cos, cos], axis=1)
    sin2 = jnp.concatenate([-sin, sin], axis=1)
    dmat, qd, kd, sd = _retention_consts(RET_CHUNK)
    row = lambda bi, r: bi * n_r + r
    const3 = lambda bi, r: (0, 0, 0)
    return pl.pallas_call(
        functools.partial(_ret_prompt_kernel, n_sub=rb // RET_CHUNK),
        grid=(b, n_r),
        in_specs=[
            pl.BlockSpec((rb, RET_QK_W), lambda bi, r: (row(bi, r), 0)),
            pl.BlockSpec((rb, RET_QK_W), lambda bi, r: (row(bi, r), 1)),
            pl.BlockSpec((rb, RET_V_W), lambda bi, r: (row(bi, r), 1)),
            pl.BlockSpec((rb, RET_V_W), lambda bi, r: (row(bi, r), 2)),
            pl.BlockSpec((rb, RET_DK), lambda bi, r: (r, 0)),
            pl.BlockSpec((rb, RET_DK), lambda bi, r: (r, 0)),
            pl.BlockSpec(dmat.shape, const3),
            pl.BlockSpec(qd.shape, const3),
            pl.BlockSpec(kd.shape, const3),
            pl.BlockSpec(sd.shape, const3),
        ],
        out_specs=[
            pl.BlockSpec((rb, RET_V_W), lambda bi, r: (row(bi, r), 0)),
            pl.BlockSpec((1, RET_HEADS, RET_DK, RET_DV), lambda bi, r: (bi, 0, 0, 0)),
        ],
        out_shape=[
            jax.ShapeDtypeStruct((b * l, RET_V_W), BF16),
            jax.ShapeDtypeStruct((b, RET_HEADS, RET_DK, RET_DV), F32),
        ],
        compiler_params=_params("parallel", "arbitrary"),
        name="retention_prompt",
    )(proj, proj, proj, proj, cos2, sin2, dmat, qd, kd, sd)


def _ret_sample_kernel(rq_ref, rk_ref, rv_ref, rg_ref, cos_ref, sin_ref, dmat_ref, qd_ref, kd_ref, sd_ref, s0_ref,
                       o_ref, s_ref, *, l):
    cos = cos_ref[...]
    sin = sin_ref[...]
    first = lax.broadcasted_iota(jnp.int32, (2 * l, 1), 0) < l
    for h in range(RET_HEADS):
        qk = slice(h * RET_DK, (h + 1) * RET_DK)
        vv = slice(h * RET_DV, (h + 1) * RET_DV)
        q = rq_ref[:, qk].astype(F32)
        k = rk_ref[:, qk].astype(F32)
        q = q * cos + _half_swap(q) * sin
        k = (k * cos + _half_swap(k) * sin) * (RET_DK ** -0.5)
        v = rv_ref[:, vv]
        att = lax.dot_general(q.astype(BF16), k.astype(BF16), _NT, preferred_element_type=F32) * dmat_ref[h]
        o = jnp.dot(att.astype(BF16), v, preferred_element_type=F32)
        qdb = (q * qd_ref[h]).astype(BF16)
        kdk = k * kd_ref[h]
        sd = sd_ref[h, 0:1, :]
        s_a = s0_ref[0, h]
        s_b = s0_ref[1, h]
        o = o + jnp.where(first,
                          jnp.dot(qdb, s_a.astype(BF16), preferred_element_type=F32),
                          jnp.dot(qdb, s_b.astype(BF16), preferred_element_type=F32))
        k_a = jnp.where(first, kdk, 0.0).astype(BF16)
        k_b = jnp.where(first, 0.0, kdk).astype(BF16)
        s_ref[0, h] = s_a * sd + lax.dot_general(k_a, v, _TN, preferred_element_type=F32)
        s_ref[1, h] = s_b * sd + lax.dot_general(k_b, v, _TN, preferred_element_type=F32)
        rg = rg_ref[:, vv].astype(F32)
        o_ref[:, vv] = (o * _rms_scale(o) * (rg * jax.nn.sigmoid(rg))).astype(BF16)


def _retention_sample(proj, state, b, l, pos0):
    assert b % 2 == 0
    cos, sin = _rope_tables(pos0 + jnp.arange(l, dtype=jnp.int32), RET_DK // 2)
    cos2 = jnp.tile(jnp.concatenate([cos, cos], axis=1), (2, 1))
    sin2 = jnp.tile(jnp.concatenate([-sin, sin], axis=1), (2, 1))
    dmat, qd, kd, sd = _retention_consts(l)
    zeros = jnp.zeros_like(dmat)
    dmat2 = jnp.concatenate([jnp.concatenate([dmat, zeros], axis=2), jnp.concatenate([zeros, dmat], axis=2)], axis=1)
    qd2 = jnp.tile(qd, (1, 2, 1))
    kd2 = jnp.tile(kd, (1, 2, 1))
    rows = 2 * l
    const2 = lambda i: (0, 0)
    const3 = lambda i: (0, 0, 0)
    return pl.pallas_call(
        functools.partial(_ret_sample_kernel, l=l),
        grid=(b // 2,),
        in_specs=[
            pl.BlockSpec((rows, RET_QK_W), lambda i: (i, 0)),
            pl.BlockSpec((rows, RET_QK_W), lambda i: (i, 1)),
            pl.BlockSpec((rows, RET_V_W), lambda i: (i, 1)),
            pl.BlockSpec((rows, RET_V_W), lambda i: (i, 2)),
            pl.BlockSpec((rows, RET_DK), const2),
            pl.BlockSpec((rows, RET_DK), const2),
            pl.BlockSpec(dmat2.shape, const3),
            pl.BlockSpec(qd2.shape, const3),
            pl.BlockSpec(kd2.shape, const3),
            pl.BlockSpec(sd.shape, const3),
            pl.BlockSpec((2, RET_HEADS, RET_DK, RET_DV), lambda i: (i, 0, 0, 0)),
        ],
        out_specs=[
            pl.BlockSpec((rows, RET_V_W), lambda i: (i, 0)),
            pl.BlockSpec((2, RET_HEADS, RET_DK, RET_DV), lambda i: (i, 0, 0, 0)),
        ],
        out_shape=[
            jax.ShapeDtypeStruct((b * l, RET_V_W), BF16),
            jax.ShapeDtypeStruct((b, RET_HEADS, RET_DK, RET_DV), F32),
        ],
        compiler_params=_params("parallel"),
        name="retention_sample",
    )(proj, proj, proj, proj, cos2, sin2, dmat2, qd2, kd2, sd, state)


def _mla_proj_kernel(x_ref, qg_ref, kvg_ref, wq_ref, wk_ref, wv_ref, cos_ref, sin_ref,
                     q_ref, ckv_ref, kr_ref, *kv_refs):
    cos = cos_ref[...]
    sin = sin_ref[...]
    cq = x_ref[:, 0:Q_LORA]
    cqn = (cq * _rms_scale(cq) * qg_ref[...]).astype(BF16)
    q = jnp.dot(cqn, wq_ref[...], preferred_element_type=F32)
    for h in range(MLA_HEADS):
        lo = h * QK_PAD
        q_ref[:, lo:lo + LANES] = q[:, lo:lo + LANES].astype(BF16)
        r = q[:, lo + LANES:lo + QK_PAD]
        q_ref[:, lo + LANES:lo + QK_PAD] = (r * cos + _half_swap(r) * sin).astype(BF16)
    ckv = x_ref[:, Q_LORA:Q_LORA + KV_LORA]
    ckvn = ckv * _rms_scale(ckv) * kvg_ref[...]
    ckv_ref[...] = ckvn
    krx = x_ref[:, Q_LORA + KV_LORA:SMALL_W]
    krr = krx * cos + _half_swap(krx) * sin
    kr_ref[...] = krr[:, 0:QK_ROPE]
    if kv_refs:
        k_ref, v_ref = kv_refs
        cb = ckvn.astype(BF16)
        kn = jnp.dot(cb, wk_ref[...], preferred_element_type=F32)
        krb = krr.astype(BF16)
        for h in range(MLA_HEADS):
            lo = h * QK_PAD
            k_ref[:, lo:lo + LANES] = kn[:, h * QK_NOPE:(h + 1) * QK_NOPE].astype(BF16)
            k_ref[:, lo + LANES:lo + QK_PAD] = krb
        v_ref[...] = jnp.dot(cb, wv_ref[...], preferred_element_type=F32).astype(BF16)


def _mla_proj(small, q_norm_g, kv_norm_g, wq, wk, wv, cosq, sinq, tm, with_kv):
    t = small.shape[0]
    tm = min(tm, t)
    rowb = lambda n: pl.BlockSpec((tm, n), lambda i: (i, 0))
    full = lambda a: pl.BlockSpec(a.shape, lambda i: (0, 0))
    qg = q_norm_g.reshape(1, Q_LORA)
    kvg = kv_norm_g.reshape(1, KV_LORA)
    out_specs = [rowb(MLA_HEADS * QK_PAD), rowb(KV_LORA), rowb(QK_ROPE)]
    out_shape = [
        jax.ShapeDtypeStruct((t, MLA_HEADS * QK_PAD), BF16),
        jax.ShapeDtypeStruct((t, KV_LORA), F32),
        jax.ShapeDtypeStruct((t, QK_ROPE), F32),
    ]
    if with_kv:
        out_specs += [rowb(MLA_HEADS * QK_PAD), rowb(MLA_HEADS * V_HEAD)]
        out_shape += [
            jax.ShapeDtypeStruct((t, MLA_HEADS * QK_PAD), BF16),
            jax.ShapeDtypeStruct((t, MLA_HEADS * V_HEAD), BF16),
        ]
    return pl.pallas_call(
        _mla_proj_kernel,
        grid=(t // tm,),
        in_specs=[rowb(SMALL_W), full(qg), full(kvg), full(wq), full(wk), full(wv), rowb(LANES), rowb(LANES)],
        out_specs=out_specs,
        out_shape=out_shape,
        compiler_params=_params("parallel"),
        name="mla_proj",
    )(small, qg, kvg, wq, wk, wv, cosq, sinq)


FLASH_HEADS = 2


def _flash_kernel(q_ref, k_ref, v_ref, o_ref, *, tq):
    qi = pl.program_id(2)
    qs = [q_ref[:, g * QK_PAD:(g + 1) * QK_PAD] for g in range(FLASH_HEADS)]

    def scores(j, g):
        off = pl.multiple_of(j * tq, tq)
        k = k_ref[pl.ds(off, tq), g * QK_PAD:(g + 1) * QK_PAD]
        v = v_ref[pl.ds(off, tq), g * V_HEAD:(g + 1) * V_HEAD]
        s = lax.dot_general(qs[g], k, _NT, preferred_element_type=F32) * (MLA_SCALE * math.log2(math.e))
        return s, v

    def update(carry, s, v):
        m, l, acc = carry
        m_new = jnp.maximum(m, jnp.max(s, axis=-1, keepdims=True))
        corr = jnp.exp2(m - m_new)
        p = jnp.exp2(s - m_new)
        l = l * corr + jnp.sum(p, axis=-1, keepdims=True)
        acc = acc * corr + jnp.dot(p.astype(BF16), v, preferred_element_type=F32)
        return m_new, l, acc

    def body(j, carries):
        return tuple(update(carries[g], *scores(j, g)) for g in range(FLASH_HEADS))

    init = (jnp.full((tq, 1), NEG_BIG, F32), jnp.zeros((tq, 1), F32), jnp.zeros((tq, V_HEAD), F32))
    carries = lax.fori_loop(0, qi, body, (init,) * FLASH_HEADS)
    row = lax.broadcasted_iota(jnp.int32, (tq, tq), 0)
    col = lax.broadcasted_iota(jnp.int32, (tq, tq), 1)
    for g in range(FLASH_HEADS):
        s, v = scores(qi, g)
        _, l, acc = update(carries[g], jnp.where(col <= row, s, -jnp.inf), v)
        o_ref[:, g * V_HEAD:(g + 1) * V_HEAD] = (acc / l).astype(BF16)


def _flash_attention(q, k, v, b, l, tq):
    tq = min(tq, l)
    n_q = l // tq
    qw = FLASH_HEADS * QK_PAD
    vw = FLASH_HEADS * V_HEAD
    return pl.pallas_call(
        functools.partial(_flash_kernel, tq=tq),
        grid=(b, MLA_HEADS // FLASH_HEADS, n_q),
        in_specs=[
            pl.BlockSpec((tq, qw), lambda bi, h, i: (bi * n_q + i, h)),
            pl.BlockSpec((l, qw), lambda bi, h, i: (bi, h)),
            pl.BlockSpec((l, vw), lambda bi, h, i: (bi, h)),
        ],
        out_specs=pl.BlockSpec((tq, vw), lambda bi, h, i: (bi * n_q + i, h)),
        out_shape=jax.ShapeDtypeStruct((b * l, MLA_HEADS * V_HEAD), BF16),
        compiler_params=_params("parallel", "parallel", "arbitrary"),
        name="mla_prompt_attention",
    )(q, k, v)


def _head_matmul_kernel(x_ref, w_ref, o_ref):
    o_ref[...] = jnp.dot(x_ref[...], w_ref[0], preferred_element_type=F32).astype(o_ref.dtype)


def _head_matmul(x, w, col_stride, out_dtype):
    m = x.shape[0]
    heads, kd, n = w.shape
    return pl.pallas_call(
        _head_matmul_kernel,
        grid=(heads,),
        in_specs=[
            pl.BlockSpec((m, kd), lambda h: (0, h * col_stride)),
            pl.BlockSpec((1, kd, n), lambda h: (h, 0, 0)),
        ],
        out_specs=pl.BlockSpec((m, n), lambda h: (0, h)),
        out_shape=jax.ShapeDtypeStruct((m, heads * n), out_dtype),
        compiler_params=_params("parallel"),
        name="head_matmul",
    )(x, w)


def _decode_kernel(pt_ref, ql_ref, qr_ref, ckv_hbm, krt_hbm, cn_ref, kn_ref, o_ref,
                   m_ref, l_ref, acc_ref, kc_ref, krc_ref, kbuf, krbuf, sem, *, pg, l_new):
    b = pl.program_id(0)
    g = pl.program_id(1)
    n_b = pl.num_programs(0)
    n_g = pl.num_programs(1)
    step = b * n_g + g
    slot = step % 2

    def page_copies(bi, gi, sl):
        copies = []
        for i in range(pg):
            page = pt_ref[bi, gi * pg + i]
            rs = pl.ds(i * PAGE_SIZE, PAGE_SIZE)
            copies.append(pltpu.make_async_copy(ckv_hbm.at[0, page], kbuf.at[sl, rs, :], sem.at[sl]))
            copies.append(pltpu.make_async_copy(krt_hbm.at[0, page], krbuf.at[sl, i], sem.at[sl]))
        return copies

    def start_all(copies):
        for k, c in enumerate(copies):
            c.start(priority=k // 2 % 2)

    @pl.when(step == 0)
    def _():
        start_all(page_copies(b, g, slot))

    last_g = g == n_g - 1
    nb = jnp.where(last_g, b + 1, b)
    ng = jnp.where(last_g, 0, g + 1)

    @pl.when(step + 1 < n_b * n_g)
    def _():
        start_all(page_copies(nb, ng, 1 - slot))

    for c in page_copies(b, g, slot):
        c.wait()

    @pl.when(g == 0)
    def _():
        m_ref[...] = jnp.full_like(m_ref, NEG_BIG)
        l_ref[...] = jnp.zeros_like(l_ref)
        acc_ref[...] = jnp.zeros_like(acc_ref)

    ql = ql_ref[0]
    qr = qr_ref[0]

    def absorb(s, vals):
        m = m_ref[...]
        m_new = jnp.maximum(m, jnp.max(s, axis=-1, keepdims=True))
        corr = jnp.exp(m - m_new)
        p = jnp.exp(s - m_new[:, 0:1])
        l_ref[...] = l_ref[...] * corr + jnp.sum(p, axis=-1, keepdims=True)
        pv = acc_ref[...] * corr[:, 0:1]
        for i, val in enumerate(vals):
            w = val.shape[0]
            pv = pv + jnp.dot(p[:, i * w:(i + 1) * w].astype(BF16), val, preferred_element_type=F32)
        acc_ref[...] = pv
        m_ref[...] = m_new

    def score(ckv, kr, kr_dims):
        return (lax.dot_general(ql, ckv, _NT, preferred_element_type=F32)
                + lax.dot_general(qr, kr, kr_dims, preferred_element_type=F32)) * MLA_SCALE

    kc_ref[...] = kbuf[slot].astype(BF16)
    for i in range(pg):
        krc_ref[:, i * PAGE_SIZE:(i + 1) * PAGE_SIZE] = krbuf[slot, i].astype(BF16)
    kc = kc_ref[...]
    absorb(score(kc, krc_ref[...], _NN), [kc])

    @pl.when(g == pl.num_programs(1) - 1)
    def _():
        cn = cn_ref[0].astype(BF16)
        s_new = score(cn, kn_ref[0].astype(BF16), _NT)
        tok = lax.broadcasted_iota(jnp.int32, s_new.shape, 0) // MLA_HEADS
        col = lax.broadcasted_iota(jnp.int32, s_new.shape, 1)
        absorb(jnp.where((col <= tok) & (col < l_new), s_new, -jnp.inf), [cn])
        o_ref[0] = (acc_ref[...] / l_ref[:, 0:1]).astype(BF16)


def _decode_attention(q_lat, q_rope, cache_ckv, cache_krope_t, page_table, ckv_new, kr_new, pg):
    b, rows, _ = q_lat.shape
    n_pages = page_table.shape[1]
    pg = min(pg, n_pages)
    l_new = ckv_new.shape[1]
    pad = 16 - l_new
    cn = jnp.pad(ckv_new, ((0, 0), (0, pad), (0, 0)))
    kn = jnp.pad(kr_new, ((0, 0), (0, pad), (0, 0)))
    per_b = lambda n: pl.BlockSpec((1, n[0], n[1]), lambda bi, g, pt: (bi, 0, 0))
    hbm = pl.BlockSpec(memory_space=pl.ANY)
    keys = pg * PAGE_SIZE
    grid_spec = pltpu.PrefetchScalarGridSpec(
        num_scalar_prefetch=1,
        grid=(b, n_pages // pg),
        in_specs=[per_b((rows, KV_LORA)), per_b((rows, QK_ROPE)), hbm, hbm,
                  per_b((16, KV_LORA)), per_b((16, QK_ROPE))],
        out_specs=per_b((rows, KV_LORA)),
        scratch_shapes=[
            pltpu.VMEM((rows, LANES), F32),
            pltpu.VMEM((rows, LANES), F32),
            pltpu.VMEM((rows, KV_LORA), F32),
            pltpu.VMEM((keys, KV_LORA), BF16),
            pltpu.VMEM((QK_ROPE, keys), BF16),
            pltpu.VMEM((2, keys, KV_LORA), F32),
            pltpu.VMEM((2, pg, QK_ROPE, PAGE_SIZE), F32),
            pltpu.SemaphoreType.DMA((2,)),
        ],
    )
    return pl.pallas_call(
        functools.partial(_decode_kernel, pg=pg, l_new=l_new),
        grid_spec=grid_spec,
        out_shape=jax.ShapeDtypeStruct((b, rows, KV_LORA), BF16),
        compiler_params=_params("arbitrary", "arbitrary"),
        name="mla_decode_attention",
    )(page_table, q_lat, q_rope, cache_ckv, cache_krope_t, cn, kn)


def _merge_kernel(ro_ref, mo_ref, wr_ref, wm_ref, ga_ref, gb_ref, o_ref):
    t1 = jnp.dot(ro_ref[...], wr_ref[...], preferred_element_type=F32)
    t2 = jnp.dot(mo_ref[...], wm_ref[...], preferred_element_type=F32)
    ga = jax.nn.sigmoid(ga_ref[...].astype(F32))
    gb = jax.nn.sigmoid(gb_ref[...].astype(F32))
    o_ref[...] = (ga * t1 + gb * t2).astype(BF16)


def _merge(ro, mo, w_ret_o, w_mla_o, proj, tm, tn):
    t = ro.shape[0]
    tm = min(tm, t)
    ga0 = (2 * RET_QK_W + 2 * RET_V_W) // tn
    gb0 = ga0 + D_MODEL // tn
    return pl.pallas_call(
        _merge_kernel,
        grid=(t // tm, D_MODEL // tn),
        in_specs=[
            pl.BlockSpec((tm, RET_V_W), lambda i, j: (i, 0)),
            pl.BlockSpec((tm, MLA_HEADS * V_HEAD), lambda i, j: (i, 0)),
            pl.BlockSpec((RET_V_W, tn), lambda i, j: (0, j)),
            pl.BlockSpec((MLA_HEADS * V_HEAD, tn), lambda i, j: (0, j)),
            pl.BlockSpec((tm, tn), lambda i, j: (i, ga0 + j)),
            pl.BlockSpec((tm, tn), lambda i, j: (i, gb0 + j)),
        ],
        out_specs=pl.BlockSpec((tm, tn), lambda i, j: (i, j)),
        out_shape=jax.ShapeDtypeStruct((t, D_MODEL), BF16),
        compiler_params=_params("parallel", "arbitrary"),
        name="branch_merge",
    )(ro, mo, w_ret_o, w_mla_o, proj, proj)


def _oproj_kernel(x_ref, m_ref, w_ref, g_ref, h_ref, xn_ref, xnt_ref):
    h = x_ref[...] + jnp.dot(m_ref[...], w_ref[...], preferred_element_type=F32)
    h_ref[...] = h
    xn = h * _rms_scale(h) * g_ref[...]
    xn_ref[...] = xn.astype(BF16)
    xnt_ref[...] = xn.T.astype(BF16)


def _oproj(x, merged, w_o, norm2_g, tm):
    t = x.shape[0]
    tm = min(tm, t)
    rowb = pl.BlockSpec((tm, D_MODEL), lambda i: (i, 0))
    return pl.pallas_call(
        _oproj_kernel,
        grid=(t // tm,),
        in_specs=[rowb, rowb, pl.BlockSpec((D_MODEL, D_MODEL), lambda i: (0, 0)),
                  pl.BlockSpec((1, D_MODEL), lambda i: (0, 0))],
        out_specs=[rowb, rowb, pl.BlockSpec((D_MODEL, tm), lambda i: (0, i))],
        out_shape=[jax.ShapeDtypeStruct((t, D_MODEL), F32), jax.ShapeDtypeStruct((t, D_MODEL), BF16),
                   jax.ShapeDtypeStruct((D_MODEL, t), BF16)],
        compiler_params=_params("parallel"),
        name="out_proj_residual",
    )(x, merged, w_o, norm2_g.reshape(1, D_MODEL))


def _ple_kernel(h_ref, y_ref, p_ref, g3_ref, wg_ref, wp_ref, gf_ref, o_ref):
    h = h_ref[...] + y_ref[...]
    n3 = (h * _rms_scale(h) * g3_ref[...]).astype(BF16)
    gate = jax.nn.sigmoid(jnp.dot(n3, wg_ref[...], preferred_element_type=F32))
    pe = jnp.dot(p_ref[...].astype(BF16), wp_ref[...], preferred_element_type=F32)
    h = h + gate * pe
    o_ref[...] = h * _rms_scale(h) * gf_ref[...]


def _ple(h, y, p_emb, norm3_g, w_gate, w_ple, final_g, tm):
    t = h.shape[0]
    tm = min(tm, t)
    rowb = pl.BlockSpec((tm, D_MODEL), lambda i: (i, 0))
    vec = pl.BlockSpec((1, D_MODEL), lambda i: (0, 0))
    return pl.pallas_call(
        _ple_kernel,
        grid=(t // tm,),
        in_specs=[rowb, rowb, pl.BlockSpec((tm, PLE_DIM), lambda i: (i, 0)), vec,
                  pl.BlockSpec((D_MODEL, D_MODEL), lambda i: (0, 0)),
                  pl.BlockSpec((PLE_DIM, D_MODEL), lambda i: (0, 0)), vec],
        out_specs=rowb,
        out_shape=jax.ShapeDtypeStruct((t, D_MODEL), F32),
        compiler_params=_params("parallel"),
        name="ple_gate_final_norm",
    )(h, y, p_emb, norm3_g.reshape(1, D_MODEL), w_gate, w_ple, final_g.reshape(1, D_MODEL))


def _top_rows(w, n):
    rows = []
    for k in range(n):
        m = jnp.max(w, axis=0, keepdims=True)
        rows.append(m)
        if k + 1 < n:
            w = jnp.where(w == m, -jnp.inf, w)
    return rows


def _peer_score_kernel(xn_ref, wq_ref, k0_ref, k1_ref, d_ref, ea_ref, b_ref, eb_ref):
    qp = jnp.dot(xn_ref[...], wq_ref[...], preferred_element_type=F32).astype(BF16)
    rank = lax.broadcasted_iota(jnp.int32, (PEER_TOPK, 1), 0)
    n_lead = 4
    for h in range(PEER_HEADS):
        qh = qp[:, h * PEER_KDIM:(h + 1) * PEER_KDIM]
        a = lax.dot_general(k0_ref[h], qh, _NT, preferred_element_type=F32)
        b = lax.dot_general(k1_ref[h], qh, _NT, preferred_element_type=F32)
        ra = _top_rows(a, PEER_TOPK + 1)
        rb = _top_rows(b, PEER_TOPK + 1)
        sa = jnp.concatenate(ra[:PEER_TOPK], axis=0)
        sb = jnp.concatenate(rb[:PEER_TOPK], axis=0)
        sa_tail = jnp.where(rank < n_lead, -jnp.inf, sa)
        cand = jnp.concatenate([ra[i] + sb for i in range(n_lead)] + [rb[i] + sa_tail for i in range(n_lead)], axis=0)
        top = _top_rows(cand, PEER_TOPK + 1)
        v16 = top[PEER_TOPK - 1]
        v17 = jnp.maximum(top[PEER_TOPK], jnp.maximum(ra[0] + rb[PEER_TOPK], rb[0] + ra[PEER_TOPK]))
        cut = jnp.where(v17 == -jnp.inf, v16, 0.5 * (v16 + v17))
        vmax = ra[0] + rb[0]
        z = jnp.sum(jnp.where(cand >= cut, jnp.exp(cand - vmax), 0.0), axis=0, keepdims=True)
        d_ref[h] = cut - a
        ea_ref[h] = jnp.exp(a - ra[0]) / z
        b_ref[h] = b
        eb_ref[h] = jnp.exp(b - rb[0])


def _peer_scores(xn, wq, k0, k1, tb):
    t = xn.shape[0]
    tb = min(tb, t)
    tab = pl.BlockSpec((PEER_HEADS, N_KEYS, tb), lambda i: (0, 0, i))
    tab_shape = jax.ShapeDtypeStruct((PEER_HEADS, N_KEYS, t), F32)
    full3 = lambda a: pl.BlockSpec(a.shape, lambda i: (0, 0, 0))
    return pl.pallas_call(
        _peer_score_kernel,
        grid=(t // tb,),
        in_specs=[pl.BlockSpec((tb, D_MODEL), lambda i: (i, 0)),
                  pl.BlockSpec(wq.shape, lambda i: (0, 0)), full3(k0), full3(k1)],
        out_specs=[tab, tab, tab, tab],
        out_shape=[tab_shape, tab_shape, tab_shape, tab_shape],
        compiler_params=_params("parallel"),
        name="peer_scores",
    )(xn, wq, k0, k1)


PEER_JQ = 32


def _peer_dense_kernel(xnt_ref, u_ref, vt_ref, d_ref, ea_ref, b_ref, eb_ref, o_ref, at_ref, w_ref, acc_ref, *, tb, eb):
    e = pl.program_id(1)

    @pl.when(e == 0)
    def _():
        acc_ref[...] = jnp.zeros_like(acc_ref)

    at_ref[...] = jnp.dot(u_ref[...], xnt_ref[...], preferred_element_type=F32)

    n_i = eb // N_KEYS

    def gate_rows(jq, carry):
        j0 = pl.multiple_of(jq * PEER_JQ, PEER_JQ)
        for tl in range(tb // LANES):
            ls = slice(tl * LANES, (tl + 1) * LANES)
            gate = [jnp.zeros((PEER_JQ, LANES), F32) for _ in range(n_i)]
            for h in range(PEER_HEADS):
                bq = b_ref[h, pl.ds(j0, PEER_JQ), ls]
                ebq = eb_ref[h, pl.ds(j0, PEER_JQ), ls]
                for ii in range(n_i):
                    hit = bq >= d_ref[h, ii:ii + 1, ls]
                    gate[ii] = jnp.where(hit, gate[ii] + ebq * ea_ref[h, ii:ii + 1, ls], gate[ii])
            for ii in range(n_i):
                r = pl.multiple_of(ii * N_KEYS + j0, PEER_JQ)
                at = at_ref[pl.ds(r, PEER_JQ), ls]
                gelu = 0.5 * at * (1.0 + lax.erf(at * math.sqrt(0.5)))
                w_ref[pl.ds(r, PEER_JQ), ls] = (gate[ii] * gelu).astype(BF16)
        return carry

    lax.fori_loop(0, N_KEYS // PEER_JQ, gate_rows, 0)

    acc_ref[...] += jnp.dot(vt_ref[...], w_ref[...], preferred_element_type=F32)

    @pl.when(e == pl.num_programs(1) - 1)
    def _():
        o_ref[...] = acc_ref[...].T


def _peer_dense(xnt, u, vt, d, ea, b, eb_tab, tb, eb):
    t = xnt.shape[1]
    tb = min(tb, t)
    rows = eb // N_KEYS
    col_tab = pl.BlockSpec((PEER_HEADS, N_KEYS, tb), lambda i, e: (0, 0, i))
    row_tab = pl.BlockSpec((PEER_HEADS, rows, tb), lambda i, e: (0, e, i))
    return pl.pallas_call(
        functools.partial(_peer_dense_kernel, tb=tb, eb=eb),
        grid=(t // tb, u.shape[0] // eb),
        in_specs=[
            pl.BlockSpec((D_MODEL, tb), lambda i, e: (0, i)),
            pl.BlockSpec((eb, D_MODEL), lambda i, e: (e, 0)),
            pl.BlockSpec((D_MODEL, eb), lambda i, e: (0, e)),
            row_tab, row_tab, col_tab, col_tab,
        ],
        out_specs=pl.BlockSpec((tb, D_MODEL), lambda i, e: (i, 0)),
        out_shape=jax.ShapeDtypeStruct((t, D_MODEL), F32),
        scratch_shapes=[
            pltpu.VMEM((eb, tb), F32),
            pltpu.VMEM((eb, tb), BF16),
            pltpu.VMEM((D_MODEL, tb), F32),
        ],
        compiler_params=_params("parallel", "arbitrary"),
        name="peer_dense",
    )(xnt, u, vt, d, ea, b, eb_tab)


def _transpose_cast_kernel(x_ref, o_ref):
    o_ref[...] = x_ref[...].T.astype(BF16)


def _transpose_cast(x, tr, tc):
    r, c = x.shape
    return pl.pallas_call(
        _transpose_cast_kernel,
        grid=(r // tr, c // tc),
        in_specs=[pl.BlockSpec((tr, tc), lambda i, j: (i, j))],
        out_specs=pl.BlockSpec((tc, tr), lambda i, j: (j, i)),
        out_shape=jax.ShapeDtypeStruct((c, r), BF16),
        compiler_params=_params("parallel", "parallel"),
        name="transpose_cast",
    )(x)


def _rope_swap(w):
    half = w.shape[-1] // 2
    return jnp.concatenate([-w[..., half:], w[..., :half]], axis=-1)


def _prepare_weights(w_in, w_uq, w_ukv, peer_keys):
    c_rg = 2 * RET_QK_W + 2 * RET_V_W
    c_ckv = c_rg + Q_LORA
    c_kr = c_ckv + KV_LORA
    c_ga = c_kr + QK_ROPE
    w_big = jnp.concatenate([w_in[:, :c_rg], w_in[:, c_ga:]], axis=1).astype(BF16)
    w_kr = w_in[:, c_kr:c_ga]
    w_small = jnp.concatenate([w_in[:, c_rg:c_kr], w_kr, _rope_swap(w_kr)], axis=1).astype(BF16)
    q_rope = w_uq[:, :, QK_NOPE:]
    wq = jnp.concatenate([w_uq[:, :, :QK_NOPE], q_rope, _rope_swap(q_rope)], axis=-1)
    wq = wq.reshape(Q_LORA, MLA_HEADS * QK_PAD).astype(BF16)
    wk = w_ukv[:, :, :QK_NOPE].reshape(KV_LORA, MLA_HEADS * QK_NOPE).astype(BF16)
    wv = w_ukv[:, :, QK_NOPE:].reshape(KV_LORA, MLA_HEADS * V_HEAD).astype(BF16)
    w_uk_t = jnp.transpose(w_ukv[:, :, :QK_NOPE], (1, 2, 0)).astype(BF16)
    w_uv = jnp.transpose(w_ukv[:, :, QK_NOPE:], (1, 0, 2)).astype(BF16)
    zeros = jnp.zeros((PEER_HEADS, N_KEYS, PEER_HALF), F32)
    k0 = jnp.concatenate([peer_keys[:, 0], zeros], axis=-1).astype(BF16)
    k1 = jnp.concatenate([zeros, peer_keys[:, 1]], axis=-1).astype(BF16)
    return w_big, w_small, wq, wk, wv, w_uk_t, w_uv, k0, k1


def _q_rope_tables(pos):
    cos, sin = _rope_tables(pos, QK_ROPE // 2)
    zeros = jnp.zeros((pos.shape[0], LANES - QK_ROPE), F32)
    return jnp.concatenate([cos, cos, zeros], axis=1), jnp.concatenate([sin, sin, zeros], axis=1)


def kernel(x_prompt, x_sample, cache_ckv, cache_krope, state_ret, page_table, p_prompt, p_sample, norm1_g, w_in, q_norm_g, w_uq, kv_norm_g, w_ukv, w_ret_o, w_mla_o, w_o, norm2_g, peer_wq, peer_keys, peer_u, peer_v, norm3_g, w_ple_gate, w_ple, final_norm_g):
    b_p, l_p, _ = x_prompt.shape
    b_s, l_s, _ = x_sample.shape
    depth = w_in.shape[0]
    assert depth == 1
    past_len = page_table.shape[1] * PAGE_SIZE
    t_p = b_p * l_p
    t_s = b_s * l_s

    w_big, w_small, wq, wk, wv, w_uk_t, w_uv, k0, k1 = _prepare_weights(w_in[0], w_uq[0], w_ukv[0], peer_keys[0])
    w_ret_o_b = w_ret_o[0].astype(BF16)
    w_mla_o_b = w_mla_o[0].astype(BF16)
    w_o_b = w_o[0].astype(BF16)
    peer_wq_b = peer_wq[0].astype(BF16)
    u_b = peer_u[0].astype(BF16)
    vt_b = _transpose_cast(peer_v[0], 1024, 512)
    w_gate_b = w_ple_gate[0].astype(BF16)
    w_ple_b = w_ple[0].astype(BF16)

    xp = x_prompt.reshape(t_p, D_MODEL)
    xs = x_sample.reshape(t_s, D_MODEL)
    pos_p = jnp.arange(l_p, dtype=jnp.int32)
    pos_s = past_len + jnp.arange(l_s, dtype=jnp.int32)

    big_p = _norm_matmul(xp, norm1_g[0], w_big, BF16, tm=512, tn=2048)
    big_s = _norm_matmul(xs, norm1_g[0], w_big, BF16, tm=512, tn=2048)
    small_p = _norm_matmul(xp, norm1_g[0], w_small, F32, tm=512, tn=SMALL_W)
    small_s = _norm_matmul(xs, norm1_g[0], w_small, F32, tm=512, tn=SMALL_W)

    ro_p, state_p = _retention_prompt(big_p, b_p, l_p, rb=256)
    ro_s, state_s = _retention_sample(big_s, state_ret[0], b_s, l_s, past_len)

    cos_p, sin_p = _q_rope_tables(pos_p)
    cos_p = jnp.tile(cos_p, (b_p, 1))
    sin_p = jnp.tile(sin_p, (b_p, 1))
    q_p, ckv_p, kr_p, k_p, v_p = _mla_proj(small_p, q_norm_g[0], kv_norm_g[0], wq, wk, wv, cos_p, sin_p, 256, True)
    mo_p = _flash_attention(q_p, k_p, v_p, b_p, l_p, tq=512)

    cos_s, sin_s = _q_rope_tables(pos_s)
    cos_s = jnp.tile(cos_s, (b_s, 1))
    sin_s = jnp.tile(sin_s, (b_s, 1))
    q_s, ckv_s, kr_s = _mla_proj(small_s, q_norm_g[0], kv_norm_g[0], wq, wk, wv, cos_s, sin_s, 256, False)
    q_lat = _head_matmul(q_s, w_uk_t, 2, BF16).reshape(b_s, l_s * MLA_HEADS, KV_LORA)
    q_rope = q_s.reshape(t_s, MLA_HEADS, QK_PAD)[:, :, QK_NOPE:QK_NOPE + QK_ROPE].reshape(b_s, l_s * MLA_HEADS, QK_ROPE)
    o_lat = _decode_attention(q_lat, q_rope, cache_ckv, jnp.swapaxes(cache_krope, 2, 3), page_table,
                              ckv_s.reshape(b_s, l_s, KV_LORA), kr_s.reshape(b_s, l_s, QK_ROPE), pg=16)
    mo_s = _head_matmul(o_lat.reshape(t_s, MLA_HEADS * KV_LORA), w_uv, 1, BF16)

    outs = []
    for x, big, ro, mo, p_emb in ((xp, big_p, ro_p, mo_p, p_prompt[0]), (xs, big_s, ro_s, mo_s, p_sample[0])):
        t = x.shape[0]
        merged = _merge(ro, mo, w_ret_o_b, w_mla_o_b, big, tm=512, tn=512)
        h, xn, xnt = _oproj(x, merged, w_o_b, norm2_g[0], tm=256)
        d, ea, b, eb_tab = _peer_scores(xn, peer_wq_b, k0, k1, tb=256)
        y = _peer_dense(xnt, u_b, vt_b, d, ea, b, eb_tab, tb=512, eb=1024)
        outs.append(_ple(h, y, p_emb.reshape(t, PLE_DIM), norm3_g[0], w_gate_b, w_ple_b, final_norm_g, tm=256))

    y_prompt = outs[0].reshape(b_p, l_p, D_MODEL)
    y_sample = outs[1].reshape(b_s, l_s, D_MODEL)
    return (y_prompt, y_sample,
            ckv_p.reshape(1, b_p, l_p, KV_LORA), kr_p.reshape(1, b_p, l_p, QK_ROPE), state_p[None],
            ckv_s.reshape(1, b_s, l_s, KV_LORA), kr_s.reshape(1, b_s, l_s, QK_ROPE), state_s[None])
```

```python
import functools
import math

import jax
import jax.numpy as jnp
from jax import lax
from jax.experimental import pallas as pl
from jax.experimental.pallas import tpu as pltpu

F32 = jnp.float32
BF16 = jnp.bfloat16

D_MODEL = 2048
RET_HEADS = 8
RET_DK = 128
RET_DV = 256
RET_CHUNK = 128
RET_QK_W = RET_HEADS * RET_DK
RET_V_W = RET_HEADS * RET_DV
MLA_HEADS = 16
Q_LORA = 768
KV_LORA = 512
QK_NOPE = 128
QK_ROPE = 64
V_HEAD = 128
MLA_SCALE = (QK_NOPE + QK_ROPE) ** -0.5
ROPE_THETA = 10000.0
PEER_HEADS = 8
N_KEYS = 128
PEER_KDIM = 128
PEER_HALF = PEER_KDIM // 2
PEER_TOPK = 16
PLE_DIM = 256
EPS = 1e-6
PAGE_SIZE = 128

LANES = 128
QK_PAD = 2 * LANES
SMALL_W = Q_LORA + KV_LORA + LANES
BIG_W = 2 * RET_QK_W + 2 * RET_V_W + 2 * D_MODEL
VMEM_LIMIT = 56 * 1024 * 1024
NEG_BIG = -1e30

TILES = dict(
    in_proj_rows=1024, in_proj_cols=2048,
    retention_rows=256,
    mla_proj_rows=256,
    flash_rows=1024,
    decode_pages=16,
    merge_rows=1024, merge_cols=512,
    out_proj_rows=256,
    peer_score_tokens=256,
    peer_tokens=512, peer_experts=1024,
    ple_rows=256,
)

_NT = (((1,), (1,)), ((), ()))
_TN = (((0,), (0,)), ((), ()))
_NN = (((1,), (0,)), ((), ()))


def _params(*sem):
    return pltpu.CompilerParams(dimension_semantics=sem, vmem_limit_bytes=VMEM_LIMIT)


def _rms_scale(x):
    return lax.rsqrt(jnp.mean(x * x, axis=-1, keepdims=True) + EPS)


def _half_swap(x):
    return pltpu.roll(x, LANES // 2, 1)


def _norm_matmul_kernel(x_ref, g_ref, w_ref, o_ref, xn_ref):
    @pl.when(pl.program_id(1) == 0)
    def _():
        x = x_ref[...]
        xn_ref[...] = (x * _rms_scale(x) * g_ref[...]).astype(BF16)

    o_ref[...] = jnp.dot(xn_ref[...], w_ref[...], preferred_element_type=F32).astype(o_ref.dtype)


def _norm_matmul(x, g, w, out_dtype, tm, tn):
    t, d = x.shape
    n = w.shape[1]
    tm = min(tm, t)
    return pl.pallas_call(
        _norm_matmul_kernel,
        grid=(t // tm, n // tn),
        in_specs=[
            pl.BlockSpec((tm, d), lambda i, j: (i, 0)),
            pl.BlockSpec((1, d), lambda i, j: (0, 0)),
            pl.BlockSpec((d, tn), lambda i, j: (0, j)),
        ],
        out_specs=pl.BlockSpec((tm, tn), lambda i, j: (i, j)),
        out_shape=jax.ShapeDtypeStruct((t, n), out_dtype),
        scratch_shapes=[pltpu.VMEM((tm, d), BF16)],
        compiler_params=_params("parallel", "arbitrary"),
        name="norm_matmul",
    )(x, g.reshape(1, d), w)


def _retention_consts(c):
    lg = jnp.log(1.0 - 2.0 ** (-5.0 - jnp.arange(RET_HEADS, dtype=F32)))
    idx = jnp.arange(c, dtype=F32)
    diff = idx[:, None] - idx[None, :]
    causal = diff >= 0
    dmat = jnp.where(causal[None], jnp.exp(lg[:, None, None] * jnp.where(causal, diff, 0.0)[None]), 0.0)
    q_dec = jnp.exp(lg[:, None] * (idx[None, :] + 1.0))
    k_dec = jnp.exp(lg[:, None] * (c - 1.0 - idx[None, :]))
    s_dec = jnp.exp(lg * c)
    qd = jnp.broadcast_to(q_dec[:, :, None], (RET_HEADS, c, RET_DK))
    kd = jnp.broadcast_to(k_dec[:, :, None], (RET_HEADS, c, RET_DK))
    sd = jnp.broadcast_to(s_dec[:, None, None], (RET_HEADS, 8, RET_DV))
    return dmat, qd, kd, sd


def _rope_tables(pos, half):
    inv = ROPE_THETA ** (-jnp.arange(half, dtype=F32) / half)
    ang = pos.astype(F32)[:, None] * inv[None, :]
    return jnp.cos(ang), jnp.sin(ang)


def _ret_head(q, k, v, s, dmat, qd, kd, sd):
    qb = q.astype(BF16)
    kb = k.astype(BF16)
    att = lax.dot_general(qb, kb, _NT, preferred_element_type=F32) * dmat
    o = jnp.dot(att.astype(BF16), v, preferred_element_type=F32)
    o = o + jnp.dot((q * qd).astype(BF16), s.astype(BF16), preferred_element_type=F32)
    s_new = s * sd + lax.dot_general((k * kd).astype(BF16), v, _TN, preferred_element_type=F32)
    return o, s_new


def _ret_prompt_kernel(rq_ref, rk_ref, rv_ref, rg_ref, cos_ref, sin_ref, dmat_ref, qd_ref, kd_ref, sd_ref,
                       o_ref, s_ref, *, n_sub):
    @pl.when(pl.program_id(1) == 0)
    def _():
        s_ref[...] = jnp.zeros_like(s_ref)

    c = RET_CHUNK
    for sub in range(n_sub):
        rows = slice(sub * c, (sub + 1) * c)
        cos = cos_ref[rows, :]
        sin = sin_ref[rows, :]
        for h in range(RET_HEADS):
            qk = slice(h * RET_DK, (h + 1) * RET_DK)
            vv = slice(h * RET_DV, (h + 1) * RET_DV)
            q = rq_ref[rows, qk].astype(F32)
            k = rk_ref[rows, qk].astype(F32)
            q = q * cos + _half_swap(q) * sin
            k = (k * cos + _half_swap(k) * sin) * (RET_DK ** -0.5)
            o, s_new = _ret_head(q, k, rv_ref[rows, vv], s_ref[0, h], dmat_ref[h], qd_ref[h], kd_ref[h],
                                 sd_ref[h, 0:1, :])
            s_ref[0, h] = s_new
            rg = rg_ref[rows, vv].astype(F32)
            o_ref[rows, vv] = (o * _rms_scale(o) * (rg * jax.nn.sigmoid(rg))).astype(BF16)


def _retention_prompt(proj, b, l, rb):
    rb = min(rb, l)
    n_r = l // rb
    cos, sin = _rope_tables(jnp.arange(l, dtype=jnp.int32), RET_DK // 2)
    cos2 = jnp.concatenate([cos, cos], axis=1)
    sin2 = jnp.concatenate([-sin, sin], axis=1)
    dmat, qd, kd, sd = _retention_consts(RET_CHUNK)
    row = lambda bi, r: bi * n_r + r
    const3 = lambda bi, r: (0, 0, 0)
    return pl.pallas_call(
        functools.partial(_ret_prompt_kernel, n_sub=rb // RET_CHUNK),
        grid=(b, n_r),
        in_specs=[
            pl.BlockSpec((rb, RET_QK_W), lambda bi, r: (row(bi, r), 0)),
            pl.BlockSpec((rb, RET_QK_W), lambda bi, r: (row(bi, r), 1)),
            pl.BlockSpec((rb, RET_V_W), lambda bi, r: (row(bi, r), 1)),
            pl.BlockSpec((rb, RET_V_W), lambda bi, r: (row(bi, r), 2)),
            pl.BlockSpec((rb, RET_DK), lambda bi, r: (r, 0)),
            pl.BlockSpec((rb, RET_DK), lambda bi, r: (r, 0)),
            pl.BlockSpec(dmat.shape, const3),
            pl.BlockSpec(qd.shape, const3),
            pl.BlockSpec(kd.shape, const3),
            pl.BlockSpec(sd.shape, const3),
        ],
        out_specs=[
            pl.BlockSpec((rb, RET_V_W), lambda bi, r: (row(bi, r), 0)),
            pl.BlockSpec((1, RET_HEADS, RET_DK, RET_DV), lambda bi, r: (bi, 0, 0, 0)),
        ],
        out_shape=[
            jax.ShapeDtypeStruct((b * l, RET_V_W), BF16),
            jax.ShapeDtypeStruct((b, RET_HEADS, RET_DK, RET_DV), F32),
        ],
        compiler_params=_params("parallel", "arbitrary"),
        name="retention_prompt",
    )(proj, proj, proj, proj, cos2, sin2, dmat, qd, kd, sd)


def _ret_sample_kernel(rq_ref, rk_ref, rv_ref, rg_ref, cos_ref, sin_ref, dmat_ref, qd_ref, kd_ref, sd_ref, s0_ref,
                       o_ref, s_ref, *, l):
    cos = cos_ref[...]
    sin = sin_ref[...]
    first = lax.broadcasted_iota(jnp.int32, (2 * l, 1), 0) < l
    for h in range(RET_HEADS):
        qk = slice(h * RET_DK, (h + 1) * RET_DK)
        vv = slice(h * RET_DV, (h + 1) * RET_DV)
        q = rq_ref[:, qk].astype(F32)
        k = rk_ref[:, qk].astype(F32)
        q = q * cos + _half_swap(q) * sin
        k = (k * cos + _half_swap(k) * sin) * (RET_DK ** -0.5)
        v = rv_ref[:, vv]
        att = lax.dot_general(q.astype(BF16), k.astype(BF16), _NT, preferred_element_type=F32) * dmat_ref[h]
        o = jnp.dot(att.astype(BF16), v, preferred_element_type=F32)
        qdb = (q * qd_ref[h]).astype(BF16)
        kdk = k * kd_ref[h]
        sd = sd_ref[h, 0:1, :]
        s_a = s0_ref[0, h]
        s_b = s0_ref[1, h]
        o = o + jnp.where(first,
                          jnp.dot(qdb, s_a.astype(BF16), preferred_element_type=F32),
                          jnp.dot(qdb, s_b.astype(BF16), preferred_element_type=F32))
        k_a = jnp.where(first, kdk, 0.0).astype(BF16)
        k_b = jnp.where(first, 0.0, kdk).astype(BF16)
        s_ref[0, h] = s_a * sd + lax.dot_general(k_a, v, _TN, preferred_element_type=F32)
        s_ref[1, h] = s_b * sd + lax.dot_general(k_b, v, _TN, preferred_element_type=F32)
        rg = rg_ref[:, vv].astype(F32)
        o_ref[:, vv] = (o * _rms_scale(o) * (rg * jax.nn.sigmoid(rg))).astype(BF16)


def _retention_sample(proj, state, b, l, pos0):
    assert b % 2 == 0
    cos, sin = _rope_tables(pos0 + jnp.arange(l, dtype=jnp.int32), RET_DK // 2)
    cos2 = jnp.tile(jnp.concatenate([cos, cos], axis=1), (2, 1))
    sin2 = jnp.tile(jnp.concatenate([-sin, sin], axis=1), (2, 1))
    dmat, qd, kd, sd = _retention_consts(l)
    zeros = jnp.zeros_like(dmat)
    dmat2 = jnp.concatenate([jnp.concatenate([dmat, zeros], axis=2), jnp.concatenate([zeros, dmat], axis=2)], axis=1)
    qd2 = jnp.tile(qd, (1, 2, 1))
    kd2 = jnp.tile(kd, (1, 2, 1))
    rows = 2 * l
    const2 = lambda i: (0, 0)
    const3 = lambda i: (0, 0, 0)
    return pl.pallas_call(
        functools.partial(_ret_sample_kernel, l=l),
        grid=(b // 2,),
        in_specs=[
            pl.BlockSpec((rows, RET_QK_W), lambda i: (i, 0)),
            pl.BlockSpec((rows, RET_QK_W), lambda i: (i, 1)),
            pl.BlockSpec((rows, RET_V_W), lambda i: (i, 1)),
            pl.BlockSpec((rows, RET_V_W), lambda i: (i, 2)),
            pl.BlockSpec((rows, RET_DK), const2),
            pl.BlockSpec((rows, RET_DK), const2),
            pl.BlockSpec(dmat2.shape, const3),
            pl.BlockSpec(qd2.shape, const3),
            pl.BlockSpec(kd2.shape, const3),
            pl.BlockSpec(sd.shape, const3),
            pl.BlockSpec((2, RET_HEADS, RET_DK, RET_DV), lambda i: (i, 0, 0, 0)),
        ],
        out_specs=[
            pl.BlockSpec((rows, RET_V_W), lambda i: (i, 0)),
            pl.BlockSpec((2, RET_HEADS, RET_DK, RET_DV), lambda i: (i, 0, 0, 0)),
        ],
        out_shape=[
            jax.ShapeDtypeStruct((b * l, RET_V_W), BF16),
            jax.ShapeDtypeStruct((b, RET_HEADS, RET_DK, RET_DV), F32),
        ],
        compiler_params=_params("parallel"),
        name="retention_sample",
    )(proj, proj, proj, proj, cos2, sin2, dmat2, qd2, kd2, sd, state)


def _mla_proj_kernel(x_ref, qg_ref, kvg_ref, wq_ref, wk_ref, wv_ref, cos_ref, sin_ref,
                     q_ref, ckv_ref, kr_ref, *kv_refs):
    cos = cos_ref[...]
    sin = sin_ref[...]
    cq = x_ref[:, 0:Q_LORA]
    cqn = (cq * _rms_scale(cq) * qg_ref[...]).astype(BF16)
    q = jnp.dot(cqn, wq_ref[...], preferred_element_type=F32)
    for h in range(MLA_HEADS):
        lo = h * QK_PAD
        q_ref[:, lo:lo + LANES] = q[:, lo:lo + LANES].astype(BF16)
        r = q[:, lo + LANES:lo + QK_PAD]
        q_ref[:, lo + LANES:lo + QK_PAD] = (r * cos + _half_swap(r) * sin).astype(BF16)
    ckv = x_ref[:, Q_LORA:Q_LORA + KV_LORA]
    ckvn = ckv * _rms_scale(ckv) * kvg_ref[...]
    ckv_ref[...] = ckvn
    krx = x_ref[:, Q_LORA + KV_LORA:SMALL_W]
    krr = krx * cos + _half_swap(krx) * sin
    kr_ref[...] = krr[:, 0:QK_ROPE]
    if kv_refs:
        k_ref, v_ref = kv_refs
        cb = ckvn.astype(BF16)
        kn = jnp.dot(cb, wk_ref[...], preferred_element_type=F32)
        krb = krr.astype(BF16)
        for h in range(MLA_HEADS):
            lo = h * QK_PAD
            k_ref[:, lo:lo + LANES] = kn[:, h * QK_NOPE:(h + 1) * QK_NOPE].astype(BF16)
            k_ref[:, lo + LANES:lo + QK_PAD] = krb
        v_ref[...] = jnp.dot(cb, wv_ref[...], preferred_element_type=F32).astype(BF16)


def _mla_proj(small, q_norm_g, kv_norm_g, wq, wk, wv, cosq, sinq, tm, with_kv):
    t = small.shape[0]
    tm = min(tm, t)
    rowb = lambda n: pl.BlockSpec((tm, n), lambda i: (i, 0))
    full = lambda a: pl.BlockSpec(a.shape, lambda i: (0, 0))
    qg = q_norm_g.reshape(1, Q_LORA)
    kvg = kv_norm_g.reshape(1, KV_LORA)
    out_specs = [rowb(MLA_HEADS * QK_PAD), rowb(KV_LORA), rowb(QK_ROPE)]
    out_shape = [
        jax.ShapeDtypeStruct((t, MLA_HEADS * QK_PAD), BF16),
        jax.ShapeDtypeStruct((t, KV_LORA), F32),
        jax.ShapeDtypeStruct((t, QK_ROPE), F32),
    ]
    if with_kv:
        out_specs += [rowb(MLA_HEADS * QK_PAD), rowb(MLA_HEADS * V_HEAD)]
        out_shape += [
            jax.ShapeDtypeStruct((t, MLA_HEADS * QK_PAD), BF16),
            jax.ShapeDtypeStruct((t, MLA_HEADS * V_HEAD), BF16),
        ]
    return pl.pallas_call(
        _mla_proj_kernel,
        grid=(t // tm,),
        in_specs=[rowb(SMALL_W), full(qg), full(kvg), full(wq), full(wk), full(wv), rowb(LANES), rowb(LANES)],
        out_specs=out_specs,
        out_shape=out_shape,
        compiler_params=_params("parallel"),
        name="mla_proj",
    )(small, qg, kvg, wq, wk, wv, cosq, sinq)


FLASH_HEADS = 2


def _flash_kernel(q_ref, k_ref, v_ref, o_ref, *, tq):
    qi = pl.program_id(2)
    qs = [q_ref[:, g * QK_PAD:(g + 1) * QK_PAD] for g in range(FLASH_HEADS)]

    def scores(j, g):
        off = pl.multiple_of(j * tq, tq)
        k = k_ref[pl.ds(off, tq), g * QK_PAD:(g + 1) * QK_PAD]
        v = v_ref[pl.ds(off, tq), g * V_HEAD:(g + 1) * V_HEAD]
        s = lax.dot_general(qs[g], k, _NT, preferred_element_type=F32) * (MLA_SCALE * math.log2(math.e))
        return s, v

    def update(carry, s, v):
        m, l, acc = carry
        m_new = jnp.maximum(m, jnp.max(s, axis=-1, keepdims=True))
        corr = jnp.exp2(m - m_new)
        p = jnp.exp2(s - m_new)
        l = l * corr + jnp.sum(p, axis=-1, keepdims=True)
        acc = acc * corr + jnp.dot(p.astype(BF16), v, preferred_element_type=F32)
        return m_new, l, acc

    def body(j, carries):
        return tuple(update(carries[g], *scores(j, g)) for g in range(FLASH_HEADS))

    init = (jnp.full((tq, 1), NEG_BIG, F32), jnp.zeros((tq, 1), F32), jnp.zeros((tq, V_HEAD), F32))
    carries = lax.fori_loop(0, qi, body, (init,) * FLASH_HEADS)
    row = lax.broadcasted_iota(jnp.int32, (tq, tq), 0)
    col = lax.broadcasted_iota(jnp.int32, (tq, tq), 1)
    for g in range(FLASH_HEADS):
        s, v = scores(qi, g)
        _, l, acc = update(carries[g], jnp.where(col <= row, s, -jnp.inf), v)
        o_ref[:, g * V_HEAD:(g + 1) * V_HEAD] = (acc / l).astype(BF16)


def _flash_attention(q, k, v, b, l, tq):
    tq = min(tq, l)
    n_q = l // tq
    qw = FLASH_HEADS * QK_PAD
    vw = FLASH_HEADS * V_HEAD
    return pl.pallas_call(
        functools.partial(_flash_kernel, tq=tq),
        grid=(b, MLA_HEADS // FLASH_HEADS, n_q),
        in_specs=[
            pl.BlockSpec((tq, qw), lambda bi, h, i: (bi * n_q + i, h)),
            pl.BlockSpec((l, qw), lambda bi, h, i: (bi, h)),
            pl.BlockSpec((l, vw), lambda bi, h, i: (bi, h)),
        ],
        out_specs=pl.BlockSpec((tq, vw), lambda bi, h, i: (bi * n_q + i, h)),
        out_shape=jax.ShapeDtypeStruct((b * l, MLA_HEADS * V_HEAD), BF16),
        compiler_params=_params("parallel", "parallel", "arbitrary"),
        name="mla_prompt_attention",
    )(q, k, v)


def _head_matmul_kernel(x_ref, w_ref, o_ref):
    o_ref[...] = jnp.dot(x_ref[...], w_ref[0], preferred_element_type=F32).astype(o_ref.dtype)


def _head_matmul(x, w, col_stride, out_dtype):
    m = x.shape[0]
    heads, kd, n = w.shape
    return pl.pallas_call(
        _head_matmul_kernel,
        grid=(heads,),
        in_specs=[
            pl.BlockSpec((m, kd), lambda h: (0, h * col_stride)),
            pl.BlockSpec((1, kd, n), lambda h: (h, 0, 0)),
        ],
        out_specs=pl.BlockSpec((m, n), lambda h: (0, h)),
        out_shape=jax.ShapeDtypeStruct((m, heads * n), out_dtype),
        compiler_params=_params("parallel"),
        name="head_matmul",
    )(x, w)


def _decode_kernel(pt_ref, ql_ref, qr_ref, ckv_hbm, krt_hbm, cn_ref, kn_ref, o_ref,
                   m_ref, l_ref, acc_ref, kc_ref, krc_ref, kbuf, krbuf, sem, *, pg, l_new):
    b = pl.program_id(0)
    g = pl.program_id(1)
    n_b = pl.num_programs(0)
    n_g = pl.num_programs(1)
    step = b * n_g + g
    slot = step % 2

    def page_copies(bi, gi, sl):
        copies = []
        for i in range(pg):
            page = pt_ref[bi, gi * pg + i]
            rs = pl.ds(i * PAGE_SIZE, PAGE_SIZE)
            copies.append(pltpu.make_async_copy(ckv_hbm.at[0, page], kbuf.at[sl, rs, :], sem.at[sl]))
            copies.append(pltpu.make_async_copy(krt_hbm.at[0, page], krbuf.at[sl, i], sem.at[sl]))
        return copies

    def start_all(copies):
        for k, c in enumerate(copies):
            c.start(priority=k // 2 % 2)

    @pl.when(step == 0)
    def _():
        start_all(page_copies(b, g, slot))

    last_g = g == n_g - 1
    nb = jnp.where(last_g, b + 1, b)
    ng = jnp.where(last_g, 0, g + 1)

    @pl.when(step + 1 < n_b * n_g)
    def _():
        start_all(page_copies(nb, ng, 1 - slot))

    for c in page_copies(b, g, slot):
        c.wait()

    @pl.when(g == 0)
    def _():
        m_ref[...] = jnp.full_like(m_ref, NEG_BIG)
        l_ref[...] = jnp.zeros_like(l_ref)
        acc_ref[...] = jnp.zeros_like(acc_ref)

    ql = ql_ref[0]
    qr = qr_ref[0]

    def absorb(s, vals):
        m = m_ref[...]
        m_new = jnp.maximum(m, jnp.max(s, axis=-1, keepdims=True))
        corr = jnp.exp(m - m_new)
        p = jnp.exp(s - m_new[:, 0:1])
        l_ref[...] = l_ref[...] * corr + jnp.sum(p, axis=-1, keepdims=True)
        pv = acc_ref[...] * corr[:, 0:1]
        for i, val in enumerate(vals):
            w = val.shape[0]
            pv = pv + jnp.dot(p[:, i * w:(i + 1) * w].astype(BF16), val, preferred_element_type=F32)
        acc_ref[...] = pv
        m_ref[...] = m_new

    def score(ckv, kr, kr_dims):
        return (lax.dot_general(ql, ckv, _NT, preferred_element_type=F32)
                + lax.dot_general(qr, kr, kr_dims, preferred_element_type=F32)) * MLA_SCALE

    kc_ref[...] = kbuf[slot].astype(BF16)
    for i in range(pg):
        krc_ref[:, i * PAGE_SIZE:(i + 1) * PAGE_SIZE] = krbuf[slot, i].astype(BF16)
    kc = kc_ref[...]
    absorb(score(kc, krc_ref[...], _NN), [kc])

    @pl.when(g == pl.num_programs(1) - 1)
    def _():
        cn = cn_ref[0].astype(BF16)
        s_new = score(cn, kn_ref[0].astype(BF16), _NT)
        tok = lax.broadcasted_iota(jnp.int32, s_new.shape, 0) // MLA_HEADS
        col = lax.broadcasted_iota(jnp.int32, s_new.shape, 1)
        absorb(jnp.where((col <= tok) & (col < l_new), s_new, -jnp.inf), [cn])
        o_ref[0] = (acc_ref[...] / l_ref[:, 0:1]).astype(BF16)


def _decode_attention(q_lat, q_rope, cache_ckv, cache_krope_t, page_table, ckv_new, kr_new, pg):
    b, rows, _ = q_lat.shape
    n_pages = page_table.shape[1]
    pg = min(pg, n_pages)
    l_new = ckv_new.shape[1]
    pad = 16 - l_new
    cn = jnp.pad(ckv_new, ((0, 0), (0, pad), (0, 0)))
    kn = jnp.pad(kr_new, ((0, 0), (0, pad), (0, 0)))
    per_b = lambda n: pl.BlockSpec((1, n[0], n[1]), lambda bi, g, pt: (bi, 0, 0))
    hbm = pl.BlockSpec(memory_space=pl.ANY)
    keys = pg * PAGE_SIZE
    grid_spec = pltpu.PrefetchScalarGridSpec(
        num_scalar_prefetch=1,
        grid=(b, n_pages // pg),
        in_specs=[per_b((rows, KV_LORA)), per_b((rows, QK_ROPE)), hbm, hbm,
                  per_b((16, KV_LORA)), per_b((16, QK_ROPE))],
        out_specs=per_b((rows, KV_LORA)),
        scratch_shapes=[
            pltpu.VMEM((rows, LANES), F32),
            pltpu.VMEM((rows, LANES), F32),
            pltpu.VMEM((rows, KV_LORA), F32),
            pltpu.VMEM((keys, KV_LORA), BF16),
            pltpu.VMEM((QK_ROPE, keys), BF16),
            pltpu.VMEM((2, keys, KV_LORA), F32),
            pltpu.VMEM((2, pg, QK_ROPE, PAGE_SIZE), F32),
            pltpu.SemaphoreType.DMA((2,)),
        ],
    )
    return pl.pallas_call(
        functools.partial(_decode_kernel, pg=pg, l_new=l_new),
        grid_spec=grid_spec,
        out_shape=jax.ShapeDtypeStruct((b, rows, KV_LORA), BF16),
        compiler_params=_params("arbitrary", "arbitrary"),
        name="mla_decode_attention",
    )(page_table, q_lat, q_rope, cache_ckv, cache_krope_t, cn, kn)


def _merge_kernel(ro_ref, mo_ref, wr_ref, wm_ref, ga_ref, gb_ref, o_ref):
    t1 = jnp.dot(ro_ref[...], wr_ref[...], preferred_element_type=F32)
    t2 = jnp.dot(mo_ref[...], wm_ref[...], preferred_element_type=F32)
    ga = jax.nn.sigmoid(ga_ref[...].astype(F32))
    gb = jax.nn.sigmoid(gb_ref[...].astype(F32))
    o_ref[...] = (ga * t1 + gb * t2).astype(BF16)


def _merge(ro, mo, w_ret_o, w_mla_o, proj, tm, tn):
    t = ro.shape[0]
    tm = min(tm, t)
    ga0 = (2 * RET_QK_W + 2 * RET_V_W) // tn
    gb0 = ga0 + D_MODEL // tn
    return pl.pallas_call(
        _merge_kernel,
        grid=(t // tm, D_MODEL // tn),
        in_specs=[
            pl.BlockSpec((tm, RET_V_W), lambda i, j: (i, 0)),
            pl.BlockSpec((tm, MLA_HEADS * V_HEAD), lambda i, j: (i, 0)),
            pl.BlockSpec((RET_V_W, tn), lambda i, j: (0, j)),
            pl.BlockSpec((MLA_HEADS * V_HEAD, tn), lambda i, j: (0, j)),
            pl.BlockSpec((tm, tn), lambda i, j: (i, ga0 + j)),
            pl.BlockSpec((tm, tn), lambda i, j: (i, gb0 + j)),
        ],
        out_specs=pl.BlockSpec((tm, tn), lambda i, j: (i, j)),
        out_shape=jax.ShapeDtypeStruct((t, D_MODEL), BF16),
        compiler_params=_params("parallel", "arbitrary"),
        name="branch_merge",
    )(ro, mo, w_ret_o, w_mla_o, proj, proj)


def _oproj_kernel(x_ref, m_ref, w_ref, g_ref, h_ref, xn_ref, xnt_ref):
    h = x_ref[...] + jnp.dot(m_ref[...], w_ref[...], preferred_element_type=F32)
    h_ref[...] = h
    xn = h * _rms_scale(h) * g_ref[...]
    xn_ref[...] = xn.astype(BF16)
    xnt_ref[...] = xn.T.astype(BF16)


def _oproj(x, merged, w_o, norm2_g, tm):
    t = x.shape[0]
    tm = min(tm, t)
    rowb = pl.BlockSpec((tm, D_MODEL), lambda i: (i, 0))
    return pl.pallas_call(
        _oproj_kernel,
        grid=(t // tm,),
        in_specs=[rowb, rowb, pl.BlockSpec((D_MODEL, D_MODEL), lambda i: (0, 0)),
                  pl.BlockSpec((1, D_MODEL), lambda i: (0, 0))],
        out_specs=[rowb, rowb, pl.BlockSpec((D_MODEL, tm), lambda i: (0, i))],
        out_shape=[jax.ShapeDtypeStruct((t, D_MODEL), F32), jax.ShapeDtypeStruct((t, D_MODEL), BF16),
                   jax.ShapeDtypeStruct((D_MODEL, t), BF16)],
        compiler_params=_params("parallel"),
        name="out_proj_residual",
    )(x, merged, w_o, norm2_g.reshape(1, D_MODEL))


def _ple_kernel(h_ref, y_ref, p_ref, g3_ref, wg_ref, wp_ref, gf_ref, o_ref):
    h = h_ref[...] + y_ref[...]
    n3 = (h * _rms_scale(h) * g3_ref[...]).astype(BF16)
    gate = jax.nn.sigmoid(jnp.dot(n3, wg_ref[...], preferred_element_type=F32))
    pe = jnp.dot(p_ref[...].astype(BF16), wp_ref[...], preferred_element_type=F32)
    h = h + gate * pe
    o_ref[...] = h * _rms_scale(h) * gf_ref[...]


def _ple(h, y, p_emb, norm3_g, w_gate, w_ple, final_g, tm):
    t = h.shape[0]
    tm = min(tm, t)
    rowb = pl.BlockSpec((tm, D_MODEL), lambda i: (i, 0))
    vec = pl.BlockSpec((1, D_MODEL), lambda i: (0, 0))
    return pl.pallas_call(
        _ple_kernel,
        grid=(t // tm,),
        in_specs=[rowb, rowb, pl.BlockSpec((tm, PLE_DIM), lambda i: (i, 0)), vec,
                  pl.BlockSpec((D_MODEL, D_MODEL), lambda i: (0, 0)),
                  pl.BlockSpec((PLE_DIM, D_MODEL), lambda i: (0, 0)), vec],
        out_specs=rowb,
        out_shape=jax.ShapeDtypeStruct((t, D_MODEL), F32),
        compiler_params=_params("parallel"),
        name="ple_gate_final_norm",
    )(h, y, p_emb, norm3_g.reshape(1, D_MODEL), w_gate, w_ple, final_g.reshape(1, D_MODEL))


def _top_rows(w, n):
    rows = []
    for k in range(n):
        m = jnp.max(w, axis=0, keepdims=True)
        rows.append(m)
        if k + 1 < n:
            w = jnp.where(w == m, -jnp.inf, w)
    return rows


def _peer_score_kernel(xn_ref, wq_ref, k0_ref, k1_ref, d_ref, ea_ref, b_ref, eb_ref):
    qp = jnp.dot(xn_ref[...], wq_ref[...], preferred_element_type=F32).astype(BF16)
    rank = lax.broadcasted_iota(jnp.int32, (PEER_TOPK, 1), 0)
    n_lead = 4
    for h in range(PEER_HEADS):
        qh = qp[:, h * PEER_KDIM:(h + 1) * PEER_KDIM]
        a = lax.dot_general(k0_ref[h], qh, _NT, preferred_element_type=F32)
        b = lax.dot_general(k1_ref[h], qh, _NT, preferred_element_type=F32)
        ra = _top_rows(a, PEER_TOPK + 1)
        rb = _top_rows(b, PEER_TOPK + 1)
        sa = jnp.concatenate(ra[:PEER_TOPK], axis=0)
        sb = jnp.concatenate(rb[:PEER_TOPK], axis=0)
        sa_tail = jnp.where(rank < n_lead, -jnp.inf, sa)
        cand = jnp.concatenate([ra[i] + sb for i in range(n_lead)] + [rb[i] + sa_tail for i in range(n_lead)], axis=0)
        top = _top_rows(cand, PEER_TOPK + 1)
        v16 = top[PEER_TOPK - 1]
        v17 = jnp.maximum(top[PEER_TOPK], jnp.maximum(ra[0] + rb[PEER_TOPK], rb[0] + ra[PEER_TOPK]))
        cut = jnp.where(v17 == -jnp.inf, v16, 0.5 * (v16 + v17))
        vmax = ra[0] + rb[0]
        z = jnp.sum(jnp.where(cand >= cut, jnp.exp(cand - vmax), 0.0), axis=0, keepdims=True)
        d_ref[h] = cut - a
        ea_ref[h] = jnp.exp(a - ra[0]) / z
        b_ref[h] = b
        eb_ref[h] = jnp.exp(b - rb[0])


def _peer_scores(xn, wq, k0, k1, tb):
    t = xn.shape[0]
    tb = min(tb, t)
    tab = pl.BlockSpec((PEER_HEADS, N_KEYS, tb), lambda i: (0, 0, i))
    tab_shape = jax.ShapeDtypeStruct((PEER_HEADS, N_KEYS, t), F32)
    full3 = lambda a: pl.BlockSpec(a.shape, lambda i: (0, 0, 0))
    return pl.pallas_call(
        _peer_score_kernel,
        grid=(t // tb,),
        in_specs=[pl.BlockSpec((tb, D_MODEL), lambda i: (i, 0)),
                  pl.BlockSpec(wq.shape, lambda i: (0, 0)), full3(k0), full3(k1)],
        out_specs=[tab, tab, tab, tab],
        out_shape=[tab_shape, tab_shape, tab_shape, tab_shape],
        compiler_params=_params("parallel"),
        name="peer_scores",
    )(xn, wq, k0, k1)


PEER_JQ = 32


def _peer_dense_kernel(xnt_ref, u_ref, vt_ref, d_ref, ea_ref, b_ref, eb_ref, o_ref, at_ref, w_ref, acc_ref, *, tb, eb):
    e = pl.program_id(1)

    @pl.when(e == 0)
    def _():
        acc_ref[...] = jnp.zeros_like(acc_ref)

    at_ref[...] = jnp.dot(u_ref[...], xnt_ref[...], preferred_element_type=F32)

    n_i = eb // N_KEYS

    def gate_rows(jq, carry):
        j0 = pl.multiple_of(jq * PEER_JQ, PEER_JQ)
        for tl in range(tb // LANES):
            ls = slice(tl * LANES, (tl + 1) * LANES)
            gate = [jnp.zeros((PEER_JQ, LANES), F32) for _ in range(n_i)]
            for h in range(PEER_HEADS):
                bq = b_ref[h, pl.ds(j0, PEER_JQ), ls]
                ebq = eb_ref[h, pl.ds(j0, PEER_JQ), ls]
                for ii in range(n_i):
                    hit = bq >= d_ref[h, ii:ii + 1, ls]
                    gate[ii] = jnp.where(hit, gate[ii] + ebq * ea_ref[h, ii:ii + 1, ls], gate[ii])
            for ii in range(n_i):
                r = pl.multiple_of(ii * N_KEYS + j0, PEER_JQ)
                at = at_ref[pl.ds(r, PEER_JQ), ls]
                gelu = 0.5 * at * (1.0 + lax.erf(at * math.sqrt(0.5)))
                w_ref[pl.ds(r, PEER_JQ), ls] = (gate[ii] * gelu).astype(BF16)
        return carry

    lax.fori_loop(0, N_KEYS // PEER_JQ, gate_rows, 0)

    acc_ref[...] += jnp.dot(vt_ref[...], w_ref[...], preferred_element_type=F32)

    @pl.when(e == pl.num_programs(1) - 1)
    def _():
        o_ref[...] = acc_ref[...].T


def _peer_dense(xnt, u, vt, d, ea, b, eb_tab, tb, eb):
    t = xnt.shape[1]
    tb = min(tb, t)
    rows = eb // N_KEYS
    col_tab = pl.BlockSpec((PEER_HEADS, N_KEYS, tb), lambda i, e: (0, 0, i))
    row_tab = pl.BlockSpec((PEER_HEADS, rows, tb), lambda i, e: (0, e, i))
    return pl.pallas_call(
        functools.partial(_peer_dense_kernel, tb=tb, eb=eb),
        grid=(t // tb, u.shape[0] // eb),
        in_specs=[
            pl.BlockSpec((D_MODEL, tb), lambda i, e: (0, i)),
            pl.BlockSpec((eb, D_MODEL), lambda i, e: (e, 0)),
            pl.BlockSpec((D_MODEL, eb), lambda i, e: (0, e)),
            row_tab, row_tab, col_tab, col_tab,
        ],
        out_specs=pl.BlockSpec((tb, D_MODEL), lambda i, e: (i, 0)),
        out_shape=jax.ShapeDtypeStruct((t, D_MODEL), F32),
        scratch_shapes=[
            pltpu.VMEM((eb, tb), F32),
            pltpu.VMEM((eb, tb), BF16),
            pltpu.VMEM((D_MODEL, tb), F32),
        ],
        compiler_params=_params("parallel", "arbitrary"),
        name="peer_dense",
    )(xnt, u, vt, d, ea, b, eb_tab)


def _transpose_cast_kernel(x_ref, o_ref):
    o_ref[...] = x_ref[...].T.astype(BF16)


def _transpose_cast(x, tr, tc):
    r, c = x.shape
    return pl.pallas_call(
        _transpose_cast_kernel,
        grid=(r // tr, c // tc),
        in_specs=[pl.BlockSpec((tr, tc), lambda i, j: (i, j))],
        out_specs=pl.BlockSpec((tc, tr), lambda i, j: (j, i)),
        out_shape=jax.ShapeDtypeStruct((c, r), BF16),
        compiler_params=_params("parallel", "parallel"),
        name="transpose_cast",
    )(x)


def _rope_swap(w):
    half = w.shape[-1] // 2
    return jnp.concatenate([-w[..., half:], w[..., :half]], axis=-1)


def _prepare_weights(w_in, w_uq, w_ukv, peer_keys):
    c_rg = 2 * RET_QK_W + 2 * RET_V_W
    c_ckv = c_rg + Q_LORA
    c_kr = c_ckv + KV_LORA
    c_ga = c_kr + QK_ROPE
    w_big = jnp.concatenate([w_in[:, :c_rg], w_in[:, c_ga:]], axis=1).astype(BF16)
    w_kr = w_in[:, c_kr:c_ga]
    w_small = jnp.concatenate([w_in[:, c_rg:c_kr], w_kr, _rope_swap(w_kr)], axis=1).astype(BF16)
    q_rope = w_uq[:, :, QK_NOPE:]
    wq = jnp.concatenate([w_uq[:, :, :QK_NOPE], q_rope, _rope_swap(q_rope)], axis=-1)
    wq = wq.reshape(Q_LORA, MLA_HEADS * QK_PAD).astype(BF16)
    wk = w_ukv[:, :, :QK_NOPE].reshape(KV_LORA, MLA_HEADS * QK_NOPE).astype(BF16)
    wv = w_ukv[:, :, QK_NOPE:].reshape(KV_LORA, MLA_HEADS * V_HEAD).astype(BF16)
    w_uk_t = jnp.transpose(w_ukv[:, :, :QK_NOPE], (1, 2, 0)).astype(BF16)
    w_uv = jnp.transpose(w_ukv[:, :, QK_NOPE:], (1, 0, 2)).astype(BF16)
    zeros = jnp.zeros((PEER_HEADS, N_KEYS, PEER_HALF), F32)
    k0 = jnp.concatenate([peer_keys[:, 0], zeros], axis=-1).astype(BF16)
    k1 = jnp.concatenate([zeros, peer_keys[:, 1]], axis=-1).astype(BF16)
    return w_big, w_small, wq, wk, wv, w_uk_t, w_uv, k0, k1


def _q_rope_tables(pos):
    cos, sin = _rope_tables(pos, QK_ROPE // 2)
    zeros = jnp.zeros((pos.shape[0], LANES - QK_ROPE), F32)
    return jnp.concatenate([cos, cos, zeros], axis=1), jnp.concatenate([sin, sin, zeros], axis=1)


def kernel(x_prompt, x_sample, cache_ckv, cache_krope, state_ret, page_table, p_prompt, p_sample, norm1_g, w_in, q_norm_g, w_uq, kv_norm_g, w_ukv, w_ret_o, w_mla_o, w_o, norm2_g, peer_wq, peer_keys, peer_u, peer_v, norm3_g, w_ple_gate, w_ple, final_norm_g):
    b_p, l_p, _ = x_prompt.shape
    b_s, l_s, _ = x_sample.shape
    depth = w_in.shape[0]
    assert depth == 1
    past_len = page_table.shape[1] * PAGE_SIZE
    t_p = b_p * l_p
    t_s = b_s * l_s

    w_big, w_small, wq, wk, wv, w_uk_t, w_uv, k0, k1 = _prepare_weights(w_in[0], w_uq[0], w_ukv[0], peer_keys[0])
    w_ret_o_b = w_ret_o[0].astype(BF16)
    w_mla_o_b = w_mla_o[0].astype(BF16)
    w_o_b = w_o[0].astype(BF16)
    peer_wq_b = peer_wq[0].astype(BF16)
    u_b = peer_u[0].astype(BF16)
    vt_b = _transpose_cast(peer_v[0], 1024, 512)
    w_gate_b = w_ple_gate[0].astype(BF16)
    w_ple_b = w_ple[0].astype(BF16)

    xp = x_prompt.reshape(t_p, D_MODEL)
    xs = x_sample.reshape(t_s, D_MODEL)
    pos_p = jnp.arange(l_p, dtype=jnp.int32)
    pos_s = past_len + jnp.arange(l_s, dtype=jnp.int32)

    big_p = _norm_matmul(xp, norm1_g[0], w_big, BF16, TILES["in_proj_rows"], TILES["in_proj_cols"])
    big_s = _norm_matmul(xs, norm1_g[0], w_big, BF16, TILES["in_proj_rows"], TILES["in_proj_cols"])
    small_p = _norm_matmul(xp, norm1_g[0], w_small, F32, TILES["in_proj_rows"], SMALL_W)
    small_s = _norm_matmul(xs, norm1_g[0], w_small, F32, TILES["in_proj_rows"], SMALL_W)

    ro_p, state_p = _retention_prompt(big_p, b_p, l_p, TILES["retention_rows"])
    ro_s, state_s = _retention_sample(big_s, state_ret[0], b_s, l_s, past_len)

    cos_p, sin_p = _q_rope_tables(pos_p)
    cos_p = jnp.tile(cos_p, (b_p, 1))
    sin_p = jnp.tile(sin_p, (b_p, 1))
    q_p, ckv_p, kr_p, k_p, v_p = _mla_proj(small_p, q_norm_g[0], kv_norm_g[0], wq, wk, wv, cos_p, sin_p,
                                           TILES["mla_proj_rows"], True)
    mo_p = _flash_attention(q_p, k_p, v_p, b_p, l_p, TILES["flash_rows"])

    cos_s, sin_s = _q_rope_tables(pos_s)
    cos_s = jnp.tile(cos_s, (b_s, 1))
    sin_s = jnp.tile(sin_s, (b_s, 1))
    q_s, ckv_s, kr_s = _mla_proj(small_s, q_norm_g[0], kv_norm_g[0], wq, wk, wv, cos_s, sin_s,
                                 TILES["mla_proj_rows"], False)
    q_lat = _head_matmul(q_s, w_uk_t, 2, BF16).reshape(b_s, l_s * MLA_HEADS, KV_LORA)
    q_rope = q_s.reshape(t_s, MLA_HEADS, QK_PAD)[:, :, QK_NOPE:QK_NOPE + QK_ROPE].reshape(b_s, l_s * MLA_HEADS, QK_ROPE)
    o_lat = _decode_attention(q_lat, q_rope, cache_ckv, jnp.swapaxes(cache_krope, 2, 3), page_table,
                              ckv_s.reshape(b_s, l_s, KV_LORA), kr_s.reshape(b_s, l_s, QK_ROPE), TILES["decode_pages"])
    mo_s = _head_matmul(o_lat.reshape(t_s, MLA_HEADS * KV_LORA), w_uv, 1, BF16)

    outs = []
    for x, big, ro, mo, p_emb in ((xp, big_p, ro_p, mo_p, p_prompt[0]), (xs, big_s, ro_s, mo_s, p_sample[0])):
        t = x.shape[0]
        merged = _merge(ro, mo, w_ret_o_b, w_mla_o_b, big, TILES["merge_rows"], TILES["merge_cols"])
        h, xn, xnt = _oproj(x, merged, w_o_b, norm2_g[0], TILES["out_proj_rows"])
        d, ea, b, eb_tab = _peer_scores(xn, peer_wq_b, k0, k1, TILES["peer_score_tokens"])
        y = _peer_dense(xnt, u_b, vt_b, d, ea, b, eb_tab, TILES["peer_tokens"], TILES["peer_experts"])
        outs.append(_ple(h, y, p_emb.reshape(t, PLE_DIM), norm3_g[0], w_gate_b, w_ple_b, final_norm_g,
                         TILES["ple_rows"]))

    y_prompt = outs[0].reshape(b_p, l_p, D_MODEL)
    y_sample = outs[1].reshape(b_s, l_s, D_MODEL)
    return (y_prompt, y_sample,
            ckv_p.reshape(1, b_p, l_p, KV_LORA), kr_p.reshape(1, b_p, l_p, QK_ROPE), state_p[None],
            ckv_s.reshape(1, b_s, l_s, KV_LORA), kr_s.reshape(1, b_s, l_s, QK_ROPE), state_s[None])
```

```python
import functools
import math

import jax
import jax.numpy as jnp
from jax import lax
from jax.experimental import pallas as pl
from jax.experimental.pallas import tpu as pltpu

F32 = jnp.float32
BF16 = jnp.bfloat16

D_MODEL = 2048
RET_HEADS = 8
RET_DK = 128
RET_DV = 256
RET_CHUNK = 128
RET_QK_W = RET_HEADS * RET_DK
RET_V_W = RET_HEADS * RET_DV
MLA_HEADS = 16
Q_LORA = 768
KV_LORA = 512
QK_NOPE = 128
QK_ROPE = 64
V_HEAD = 128
MLA_SCALE = (QK_NOPE + QK_ROPE) ** -0.5
ROPE_THETA = 10000.0
PEER_HEADS = 8
N_KEYS = 128
PEER_KDIM = 128
PEER_HALF = PEER_KDIM // 2
PEER_TOPK = 16
PLE_DIM = 256
EPS = 1e-6
PAGE_SIZE = 128

LANES = 128
QK_PAD = 2 * LANES
SMALL_W = Q_LORA + KV_LORA + LANES
BIG_W = 2 * RET_QK_W + 2 * RET_V_W + 2 * D_MODEL
VMEM_LIMIT = 56 * 1024 * 1024
NEG_BIG = -1e30

TILES = dict(
    in_proj_rows=1024, in_proj_cols=2048,
    retention_rows=256,
    mla_proj_rows=256,
    flash_rows=1024,
    decode_pages=16,
    merge_rows=1024, merge_cols=512,
    out_proj_rows=512,
    peer_score_tokens=256,
    peer_tokens=512, peer_experts=1024,
    ple_rows=512,
)

_NT = (((1,), (1,)), ((), ()))
_TN = (((0,), (0,)), ((), ()))
_NN = (((1,), (0,)), ((), ()))


def _params(*sem):
    return pltpu.CompilerParams(dimension_semantics=sem, vmem_limit_bytes=VMEM_LIMIT)


def _rms_scale(x):
    return lax.rsqrt(jnp.mean(x * x, axis=-1, keepdims=True) + EPS)


def _half_swap(x):
    return pltpu.roll(x, LANES // 2, 1)


def _norm_matmul_kernel(x_ref, g_ref, w_ref, o_ref, xn_ref):
    @pl.when(pl.program_id(1) == 0)
    def _():
        x = x_ref[...]
        xn_ref[...] = (x * _rms_scale(x) * g_ref[...]).astype(BF16)

    o_ref[...] = jnp.dot(xn_ref[...], w_ref[...], preferred_element_type=F32).astype(o_ref.dtype)


def _norm_matmul(x, g, w, out_dtype, tm, tn):
    t, d = x.shape
    n = w.shape[1]
    tm = min(tm, t)
    return pl.pallas_call(
        _norm_matmul_kernel,
        grid=(t // tm, n // tn),
        in_specs=[
            pl.BlockSpec((tm, d), lambda i, j: (i, 0)),
            pl.BlockSpec((1, d), lambda i, j: (0, 0)),
            pl.BlockSpec((d, tn), lambda i, j: (0, j)),
        ],
        out_specs=pl.BlockSpec((tm, tn), lambda i, j: (i, j)),
        out_shape=jax.ShapeDtypeStruct((t, n), out_dtype),
        scratch_shapes=[pltpu.VMEM((tm, d), BF16)],
        compiler_params=_params("parallel", "arbitrary"),
        name="norm_matmul",
    )(x, g.reshape(1, d), w)


def _retention_consts(c):
    lg = jnp.log(1.0 - 2.0 ** (-5.0 - jnp.arange(RET_HEADS, dtype=F32)))
    idx = jnp.arange(c, dtype=F32)
    diff = idx[:, None] - idx[None, :]
    causal = diff >= 0
    dmat = jnp.where(causal[None], jnp.exp(lg[:, None, None] * jnp.where(causal, diff, 0.0)[None]), 0.0)
    q_dec = jnp.exp(lg[:, None] * (idx[None, :] + 1.0))
    k_dec = jnp.exp(lg[:, None] * (c - 1.0 - idx[None, :]))
    s_dec = jnp.exp(lg * c)
    qd = jnp.broadcast_to(q_dec[:, :, None], (RET_HEADS, c, RET_DK))
    kd = jnp.broadcast_to(k_dec[:, :, None], (RET_HEADS, c, RET_DK))
    sd = jnp.broadcast_to(s_dec[:, None, None], (RET_HEADS, 8, RET_DV))
    return dmat, qd, kd, sd


def _rope_tables(pos, half):
    inv = ROPE_THETA ** (-jnp.arange(half, dtype=F32) / half)
    ang = pos.astype(F32)[:, None] * inv[None, :]
    return jnp.cos(ang), jnp.sin(ang)


def _ret_head(q, k, v, s, dmat, qd, kd, sd):
    qb = q.astype(BF16)
    kb = k.astype(BF16)
    att = lax.dot_general(qb, kb, _NT, preferred_element_type=F32) * dmat
    o = jnp.dot(att.astype(BF16), v, preferred_element_type=F32)
    o = o + jnp.dot((q * qd).astype(BF16), s.astype(BF16), preferred_element_type=F32)
    s_new = s * sd + lax.dot_general((k * kd).astype(BF16), v, _TN, preferred_element_type=F32)
    return o, s_new


def _ret_prompt_kernel(rq_ref, rk_ref, rv_ref, rg_ref, cos_ref, sin_ref, dmat_ref, qd_ref, kd_ref, sd_ref,
                       o_ref, s_ref, *, n_sub):
    @pl.when(pl.program_id(1) == 0)
    def _():
        s_ref[...] = jnp.zeros_like(s_ref)

    c = RET_CHUNK
    for sub in range(n_sub):
        rows = slice(sub * c, (sub + 1) * c)
        cos = cos_ref[rows, :]
        sin = sin_ref[rows, :]
        for h in range(RET_HEADS):
            qk = slice(h * RET_DK, (h + 1) * RET_DK)
            vv = slice(h * RET_DV, (h + 1) * RET_DV)
            q = rq_ref[rows, qk].astype(F32)
            k = rk_ref[rows, qk].astype(F32)
            q = q * cos + _half_swap(q) * sin
            k = (k * cos + _half_swap(k) * sin) * (RET_DK ** -0.5)
            o, s_new = _ret_head(q, k, rv_ref[rows, vv], s_ref[0, h], dmat_ref[h], qd_ref[h], kd_ref[h],
                                 sd_ref[h, 0:1, :])
            s_ref[0, h] = s_new
            rg = rg_ref[rows, vv].astype(F32)
            o_ref[rows, vv] = (o * _rms_scale(o) * (rg * jax.nn.sigmoid(rg))).astype(BF16)


def _retention_prompt(proj, b, l, rb):
    rb = min(rb, l)
    n_r = l // rb
    cos, sin = _rope_tables(jnp.arange(l, dtype=jnp.int32), RET_DK // 2)
    cos2 = jnp.concatenate([cos, cos], axis=1)
    sin2 = jnp.concatenate([-sin, sin], axis=1)
    dmat, qd, kd, sd = _retention_consts(RET_CHUNK)
    row = lambda bi, r: bi * n_r + r
    const3 = lambda bi, r: (0, 0, 0)
    return pl.pallas_call(
        functools.partial(_ret_prompt_kernel, n_sub=rb // RET_CHUNK),
        grid=(b, n_r),
        in_specs=[
            pl.BlockSpec((rb, RET_QK_W), lambda bi, r: (row(bi, r), 0)),
            pl.BlockSpec((rb, RET_QK_W), lambda bi, r: (row(bi, r), 1)),
            pl.BlockSpec((rb, RET_V_W), lambda bi, r: (row(bi, r), 1)),
            pl.BlockSpec((rb, RET_V_W), lambda bi, r: (row(bi, r), 2)),
            pl.BlockSpec((rb, RET_DK), lambda bi, r: (r, 0)),
            pl.BlockSpec((rb, RET_DK), lambda bi, r: (r, 0)),
            pl.BlockSpec(dmat.shape, const3),
            pl.BlockSpec(qd.shape, const3),
            pl.BlockSpec(kd.shape, const3),
            pl.BlockSpec(sd.shape, const3),
        ],
        out_specs=[
            pl.BlockSpec((rb, RET_V_W), lambda bi, r: (row(bi, r), 0)),
            pl.BlockSpec((1, RET_HEADS, RET_DK, RET_DV), lambda bi, r: (bi, 0, 0, 0)),
        ],
        out_shape=[
            jax.ShapeDtypeStruct((b * l, RET_V_W), BF16),
            jax.ShapeDtypeStruct((b, RET_HEADS, RET_DK, RET_DV), F32),
        ],
        compiler_params=_params("parallel", "arbitrary"),
        name="retention_prompt",
    )(proj, proj, proj, proj, cos2, sin2, dmat, qd, kd, sd)


def _ret_sample_kernel(rq_ref, rk_ref, rv_ref, rg_ref, cos_ref, sin_ref, dmat_ref, qd_ref, kd_ref, sd_ref, s0_ref,
                       o_ref, s_ref, *, l):
    cos = cos_ref[...]
    sin = sin_ref[...]
    first = lax.broadcasted_iota(jnp.int32, (2 * l, 1), 0) < l
    for h in range(RET_HEADS):
        qk = slice(h * RET_DK, (h + 1) * RET_DK)
        vv = slice(h * RET_DV, (h + 1) * RET_DV)
        q = rq_ref[:, qk].astype(F32)
        k = rk_ref[:, qk].astype(F32)
        q = q * cos + _half_swap(q) * sin
        k = (k * cos + _half_swap(k) * sin) * (RET_DK ** -0.5)
        v = rv_ref[:, vv]
        att = lax.dot_general(q.astype(BF16), k.astype(BF16), _NT, preferred_element_type=F32) * dmat_ref[h]
        o = jnp.dot(att.astype(BF16), v, preferred_element_type=F32)
        qdb = (q * qd_ref[h]).astype(BF16)
        kdk = k * kd_ref[h]
        sd = sd_ref[h, 0:1, :]
        s_a = s0_ref[0, h]
        s_b = s0_ref[1, h]
        o = o + jnp.where(first,
                          jnp.dot(qdb, s_a.astype(BF16), preferred_element_type=F32),
                          jnp.dot(qdb, s_b.astype(BF16), preferred_element_type=F32))
        k_a = jnp.where(first, kdk, 0.0).astype(BF16)
        k_b = jnp.where(first, 0.0, kdk).astype(BF16)
        s_ref[0, h] = s_a * sd + lax.dot_general(k_a, v, _TN, preferred_element_type=F32)
        s_ref[1, h] = s_b * sd + lax.dot_general(k_b, v, _TN, preferred_element_type=F32)
        rg = rg_ref[:, vv].astype(F32)
        o_ref[:, vv] = (o * _rms_scale(o) * (rg * jax.nn.sigmoid(rg))).astype(BF16)


def _retention_sample(proj, state, b, l, pos0):
    assert b % 2 == 0
    cos, sin = _rope_tables(pos0 + jnp.arange(l, dtype=jnp.int32), RET_DK // 2)
    cos2 = jnp.tile(jnp.concatenate([cos, cos], axis=1), (2, 1))
    sin2 = jnp.tile(jnp.concatenate([-sin, sin], axis=1), (2, 1))
    dmat, qd, kd, sd = _retention_consts(l)
    zeros = jnp.zeros_like(dmat)
    dmat2 = jnp.concatenate([jnp.concatenate([dmat, zeros], axis=2), jnp.concatenate([zeros, dmat], axis=2)], axis=1)
    qd2 = jnp.tile(qd, (1, 2, 1))
    kd2 = jnp.tile(kd, (1, 2, 1))
    rows = 2 * l
    const2 = lambda i: (0, 0)
    const3 = lambda i: (0, 0, 0)
    return pl.pallas_call(
        functools.partial(_ret_sample_kernel, l=l),
        grid=(b // 2,),
        in_specs=[
            pl.BlockSpec((rows, RET_QK_W), lambda i: (i, 0)),
            pl.BlockSpec((rows, RET_QK_W), lambda i: (i, 1)),
            pl.BlockSpec((rows, RET_V_W), lambda i: (i, 1)),
            pl.BlockSpec((rows, RET_V_W), lambda i: (i, 2)),
            pl.BlockSpec((rows, RET_DK), const2),
            pl.BlockSpec((rows, RET_DK), const2),
            pl.BlockSpec(dmat2.shape, const3),
            pl.BlockSpec(qd2.shape, const3),
            pl.BlockSpec(kd2.shape, const3),
            pl.BlockSpec(sd.shape, const3),
            pl.BlockSpec((2, RET_HEADS, RET_DK, RET_DV), lambda i: (i, 0, 0, 0)),
        ],
        out_specs=[
            pl.BlockSpec((rows, RET_V_W), lambda i: (i, 0)),
            pl.BlockSpec((2, RET_HEADS, RET_DK, RET_DV), lambda i: (i, 0, 0, 0)),
        ],
        out_shape=[
            jax.ShapeDtypeStruct((b * l, RET_V_W), BF16),
            jax.ShapeDtypeStruct((b, RET_HEADS, RET_DK, RET_DV), F32),
        ],
        compiler_params=_params("parallel"),
        name="retention_sample",
    )(proj, proj, proj, proj, cos2, sin2, dmat2, qd2, kd2, sd, state)


def _mla_proj_kernel(x_ref, qg_ref, kvg_ref, wq_ref, wk_ref, wv_ref, cos_ref, sin_ref,
                     q_ref, ckv_ref, kr_ref, *kv_refs):
    cos = cos_ref[...]
    sin = sin_ref[...]
    cq = x_ref[:, 0:Q_LORA]
    cqn = (cq * _rms_scale(cq) * qg_ref[...]).astype(BF16)
    q = jnp.dot(cqn, wq_ref[...], preferred_element_type=F32)
    for h in range(MLA_HEADS):
        lo = h * QK_PAD
        q_ref[:, lo:lo + LANES] = q[:, lo:lo + LANES].astype(BF16)
        r = q[:, lo + LANES:lo + QK_PAD]
        q_ref[:, lo + LANES:lo + QK_PAD] = (r * cos + _half_swap(r) * sin).astype(BF16)
    ckv = x_ref[:, Q_LORA:Q_LORA + KV_LORA]
    ckvn = ckv * _rms_scale(ckv) * kvg_ref[...]
    ckv_ref[...] = ckvn
    krx = x_ref[:, Q_LORA + KV_LORA:SMALL_W]
    krr = krx * cos + _half_swap(krx) * sin
    kr_ref[...] = krr[:, 0:QK_ROPE]
    if kv_refs:
        k_ref, v_ref = kv_refs
        cb = ckvn.astype(BF16)
        kn = jnp.dot(cb, wk_ref[...], preferred_element_type=F32)
        krb = krr.astype(BF16)
        for h in range(MLA_HEADS):
            lo = h * QK_PAD
            k_ref[:, lo:lo + LANES] = kn[:, h * QK_NOPE:(h + 1) * QK_NOPE].astype(BF16)
            k_ref[:, lo + LANES:lo + QK_PAD] = krb
        v_ref[...] = jnp.dot(cb, wv_ref[...], preferred_element_type=F32).astype(BF16)


def _mla_proj(small, q_norm_g, kv_norm_g, wq, wk, wv, cosq, sinq, tm, with_kv):
    t = small.shape[0]
    tm = min(tm, t)
    rowb = lambda n: pl.BlockSpec((tm, n), lambda i: (i, 0))
    full = lambda a: pl.BlockSpec(a.shape, lambda i: (0, 0))
    qg = q_norm_g.reshape(1, Q_LORA)
    kvg = kv_norm_g.reshape(1, KV_LORA)
    out_specs = [rowb(MLA_HEADS * QK_PAD), rowb(KV_LORA), rowb(QK_ROPE)]
    out_shape = [
        jax.ShapeDtypeStruct((t, MLA_HEADS * QK_PAD), BF16),
        jax.ShapeDtypeStruct((t, KV_LORA), F32),
        jax.ShapeDtypeStruct((t, QK_ROPE), F32),
    ]
    if with_kv:
        out_specs += [rowb(MLA_HEADS * QK_PAD), rowb(MLA_HEADS * V_HEAD)]
        out_shape += [
            jax.ShapeDtypeStruct((t, MLA_HEADS * QK_PAD), BF16),
            jax.ShapeDtypeStruct((t, MLA_HEADS * V_HEAD), BF16),
        ]
    return pl.pallas_call(
        _mla_proj_kernel,
        grid=(t // tm,),
        in_specs=[rowb(SMALL_W), full(qg), full(kvg), full(wq), full(wk), full(wv), rowb(LANES), rowb(LANES)],
        out_specs=out_specs,
        out_shape=out_shape,
        compiler_params=_params("parallel"),
        name="mla_proj",
    )(small, qg, kvg, wq, wk, wv, cosq, sinq)


FLASH_HEADS = 2


def _flash_kernel(q_ref, k_ref, v_ref, o_ref, *, tq):
    qi = pl.program_id(2)
    qs = [q_ref[:, g * QK_PAD:(g + 1) * QK_PAD] for g in range(FLASH_HEADS)]

    def scores(q, off, n, g):
        k = k_ref[pl.ds(off, n), g * QK_PAD:(g + 1) * QK_PAD]
        v = v_ref[pl.ds(off, n), g * V_HEAD:(g + 1) * V_HEAD]
        s = lax.dot_general(q, k, _NT, preferred_element_type=F32) * (MLA_SCALE * math.log2(math.e))
        return s, v

    def update(carry, s, v):
        m, l, acc = carry
        m_new = jnp.maximum(m, jnp.max(s, axis=-1, keepdims=True))
        corr = jnp.exp2(m - m_new)
        p = jnp.exp2(s - m_new)
        l = l * corr + jnp.sum(p, axis=-1, keepdims=True)
        acc = acc * corr + jnp.dot(p.astype(BF16), v, preferred_element_type=F32)
        return m_new, l, acc

    def body(j, carries):
        off = pl.multiple_of(j * tq, tq)
        return tuple(update(carries[g], *scores(qs[g], off, tq, g)) for g in range(FLASH_HEADS))

    init = (jnp.full((tq, 1), NEG_BIG, F32), jnp.zeros((tq, 1), F32), jnp.zeros((tq, V_HEAD), F32))
    carries = lax.fori_loop(0, qi, body, (init,) * FLASH_HEADS)

    diag = pl.multiple_of(qi * tq, tq)
    row = lax.broadcasted_iota(jnp.int32, (tq, tq), 0)
    col = lax.broadcasted_iota(jnp.int32, (tq, tq), 1)
    for g in range(FLASH_HEADS):
        s, v = scores(qs[g], diag, tq, g)
        _, l, acc = update(carries[g], jnp.where(col <= row, s, -jnp.inf), v)
        o_ref[:, g * V_HEAD:(g + 1) * V_HEAD] = (acc / l).astype(BF16)


def _flash_attention(q, k, v, b, l, tq):
    tq = min(tq, l)
    n_q = l // tq
    qw = FLASH_HEADS * QK_PAD
    vw = FLASH_HEADS * V_HEAD
    return pl.pallas_call(
        functools.partial(_flash_kernel, tq=tq),
        grid=(b, MLA_HEADS // FLASH_HEADS, n_q),
        in_specs=[
            pl.BlockSpec((tq, qw), lambda bi, h, i: (bi * n_q + i, h)),
            pl.BlockSpec((l, qw), lambda bi, h, i: (bi, h)),
            pl.BlockSpec((l, vw), lambda bi, h, i: (bi, h)),
        ],
        out_specs=pl.BlockSpec((tq, vw), lambda bi, h, i: (bi * n_q + i, h)),
        out_shape=jax.ShapeDtypeStruct((b * l, MLA_HEADS * V_HEAD), BF16),
        compiler_params=_params("parallel", "parallel", "arbitrary"),
        name="mla_prompt_attention",
    )(q, k, v)


def _head_matmul_kernel(x_ref, w_ref, o_ref):
    o_ref[...] = jnp.dot(x_ref[...], w_ref[0], preferred_element_type=F32).astype(o_ref.dtype)


def _head_matmul(x, w, col_stride, out_dtype):
    m = x.shape[0]
    heads, kd, n = w.shape
    return pl.pallas_call(
        _head_matmul_kernel,
        grid=(heads,),
        in_specs=[
            pl.BlockSpec((m, kd), lambda h: (0, h * col_stride)),
            pl.BlockSpec((1, kd, n), lambda h: (h, 0, 0)),
        ],
        out_specs=pl.BlockSpec((m, n), lambda h: (0, h)),
        out_shape=jax.ShapeDtypeStruct((m, heads * n), out_dtype),
        compiler_params=_params("parallel"),
        name="head_matmul",
    )(x, w)


def _decode_kernel(pt_ref, ql_ref, qr_ref, ckv_hbm, krt_hbm, cn_ref, kn_ref, o_ref,
                   m_ref, l_ref, acc_ref, kc_ref, krc_ref, kbuf, krbuf, sem, *, pg, l_new):
    b = pl.program_id(0)
    g = pl.program_id(1)
    n_b = pl.num_programs(0)
    n_g = pl.num_programs(1)
    step = b * n_g + g
    slot = step % 2

    def page_copies(bi, gi, sl):
        copies = []
        for i in range(pg):
            page = pt_ref[bi, gi * pg + i]
            rs = pl.ds(i * PAGE_SIZE, PAGE_SIZE)
            copies.append(pltpu.make_async_copy(ckv_hbm.at[0, page], kbuf.at[sl, rs, :], sem.at[sl]))
            copies.append(pltpu.make_async_copy(krt_hbm.at[0, page], krbuf.at[sl, i], sem.at[sl]))
        return copies

    def start_all(copies):
        for k, c in enumerate(copies):
            c.start(priority=k // 2 % 2)

    @pl.when(step == 0)
    def _():
        start_all(page_copies(b, g, slot))

    last_g = g == n_g - 1
    nb = jnp.where(last_g, b + 1, b)
    ng = jnp.where(last_g, 0, g + 1)

    @pl.when(step + 1 < n_b * n_g)
    def _():
        start_all(page_copies(nb, ng, 1 - slot))

    for c in page_copies(b, g, slot):
        c.wait()

    @pl.when(g == 0)
    def _():
        m_ref[...] = jnp.full_like(m_ref, NEG_BIG)
        l_ref[...] = jnp.zeros_like(l_ref)
        acc_ref[...] = jnp.zeros_like(acc_ref)

    ql = ql_ref[0]
    qr = qr_ref[0]

    def absorb(s, vals):
        m = m_ref[...]
        m_new = jnp.maximum(m, jnp.max(s, axis=-1, keepdims=True))
        corr = jnp.exp(m - m_new)
        p = jnp.exp(s - m_new[:, 0:1])
        l_ref[...] = l_ref[...] * corr + jnp.sum(p, axis=-1, keepdims=True)
        pv = acc_ref[...] * corr[:, 0:1]
        for i, val in enumerate(vals):
            w = val.shape[0]
            pv = pv + jnp.dot(p[:, i * w:(i + 1) * w].astype(BF16), val, preferred_element_type=F32)
        acc_ref[...] = pv
        m_ref[...] = m_new

    def score(ckv, kr, kr_dims):
        return (lax.dot_general(ql, ckv, _NT, preferred_element_type=F32)
                + lax.dot_general(qr, kr, kr_dims, preferred_element_type=F32)) * MLA_SCALE

    kc_ref[...] = kbuf[slot].astype(BF16)
    for i in range(pg):
        krc_ref[:, i * PAGE_SIZE:(i + 1) * PAGE_SIZE] = krbuf[slot, i].astype(BF16)
    kc = kc_ref[...]
    absorb(score(kc, krc_ref[...], _NN), [kc])

    @pl.when(g == pl.num_programs(1) - 1)
    def _():
        cn = cn_ref[0].astype(BF16)
        s_new = score(cn, kn_ref[0].astype(BF16), _NT)
        tok = lax.broadcasted_iota(jnp.int32, s_new.shape, 0) // MLA_HEADS
        col = lax.broadcasted_iota(jnp.int32, s_new.shape, 1)
        absorb(jnp.where((col <= tok) & (col < l_new), s_new, -jnp.inf), [cn])
        o_ref[0] = (acc_ref[...] / l_ref[:, 0:1]).astype(BF16)


def _decode_attention(q_lat, q_rope, cache_ckv, cache_krope_t, page_table, ckv_new, kr_new, pg):
    b, rows, _ = q_lat.shape
    n_pages = page_table.shape[1]
    pg = min(pg, n_pages)
    l_new = ckv_new.shape[1]
    pad = 16 - l_new
    cn = jnp.pad(ckv_new, ((0, 0), (0, pad), (0, 0)))
    kn = jnp.pad(kr_new, ((0, 0), (0, pad), (0, 0)))
    per_b = lambda n: pl.BlockSpec((1, n[0], n[1]), lambda bi, g, pt: (bi, 0, 0))
    hbm = pl.BlockSpec(memory_space=pl.ANY)
    keys = pg * PAGE_SIZE
    grid_spec = pltpu.PrefetchScalarGridSpec(
        num_scalar_prefetch=1,
        grid=(b, n_pages // pg),
        in_specs=[per_b((rows, KV_LORA)), per_b((rows, QK_ROPE)), hbm, hbm,
                  per_b((16, KV_LORA)), per_b((16, QK_ROPE))],
        out_specs=per_b((rows, KV_LORA)),
        scratch_shapes=[
            pltpu.VMEM((rows, LANES), F32),
            pltpu.VMEM((rows, LANES), F32),
            pltpu.VMEM((rows, KV_LORA), F32),
            pltpu.VMEM((keys, KV_LORA), BF16),
            pltpu.VMEM((QK_ROPE, keys), BF16),
            pltpu.VMEM((2, keys, KV_LORA), F32),
            pltpu.VMEM((2, pg, QK_ROPE, PAGE_SIZE), F32),
            pltpu.SemaphoreType.DMA((2,)),
        ],
    )
    return pl.pallas_call(
        functools.partial(_decode_kernel, pg=pg, l_new=l_new),
        grid_spec=grid_spec,
        out_shape=jax.ShapeDtypeStruct((b, rows, KV_LORA), BF16),
        compiler_params=_params("arbitrary", "arbitrary"),
        name="mla_decode_attention",
    )(page_table, q_lat, q_rope, cache_ckv, cache_krope_t, cn, kn)


def _merge_kernel(ro_ref, mo_ref, wr_ref, wm_ref, ga_ref, gb_ref, o_ref):
    t1 = jnp.dot(ro_ref[...], wr_ref[...], preferred_element_type=F32)
    t2 = jnp.dot(mo_ref[...], wm_ref[...], preferred_element_type=F32)
    ga = jax.nn.sigmoid(ga_ref[...].astype(F32))
    gb = jax.nn.sigmoid(gb_ref[...].astype(F32))
    o_ref[...] = (ga * t1 + gb * t2).astype(BF16)


def _merge(ro, mo, w_ret_o, w_mla_o, proj, tm, tn):
    t = ro.shape[0]
    tm = min(tm, t)
    ga0 = (2 * RET_QK_W + 2 * RET_V_W) // tn
    gb0 = ga0 + D_MODEL // tn
    return pl.pallas_call(
        _merge_kernel,
        grid=(t // tm, D_MODEL // tn),
        in_specs=[
            pl.BlockSpec((tm, RET_V_W), lambda i, j: (i, 0)),
            pl.BlockSpec((tm, MLA_HEADS * V_HEAD), lambda i, j: (i, 0)),
            pl.BlockSpec((RET_V_W, tn), lambda i, j: (0, j)),
            pl.BlockSpec((MLA_HEADS * V_HEAD, tn), lambda i, j: (0, j)),
            pl.BlockSpec((tm, tn), lambda i, j: (i, ga0 + j)),
            pl.BlockSpec((tm, tn), lambda i, j: (i, gb0 + j)),
        ],
        out_specs=pl.BlockSpec((tm, tn), lambda i, j: (i, j)),
        out_shape=jax.ShapeDtypeStruct((t, D_MODEL), BF16),
        compiler_params=_params("parallel", "arbitrary"),
        name="branch_merge",
    )(ro, mo, w_ret_o, w_mla_o, proj, proj)


def _oproj_kernel(x_ref, m_ref, w_ref, g_ref, h_ref, xn_ref, xnt_ref):
    h = x_ref[...] + jnp.dot(m_ref[...], w_ref[...], preferred_element_type=F32)
    h_ref[...] = h
    xn = h * _rms_scale(h) * g_ref[...]
    xn_ref[...] = xn.astype(BF16)
    xnt_ref[...] = xn.T.astype(BF16)


def _oproj(x, merged, w_o, norm2_g, tm):
    t = x.shape[0]
    tm = min(tm, t)
    rowb = pl.BlockSpec((tm, D_MODEL), lambda i: (i, 0))
    return pl.pallas_call(
        _oproj_kernel,
        grid=(t // tm,),
        in_specs=[rowb, rowb, pl.BlockSpec((D_MODEL, D_MODEL), lambda i: (0, 0)),
                  pl.BlockSpec((1, D_MODEL), lambda i: (0, 0))],
        out_specs=[rowb, rowb, pl.BlockSpec((D_MODEL, tm), lambda i: (0, i))],
        out_shape=[jax.ShapeDtypeStruct((t, D_MODEL), F32), jax.ShapeDtypeStruct((t, D_MODEL), BF16),
                   jax.ShapeDtypeStruct((D_MODEL, t), BF16)],
        compiler_params=_params("parallel"),
        name="out_proj_residual",
    )(x, merged, w_o, norm2_g.reshape(1, D_MODEL))


def _ple_kernel(h_ref, y_ref, p_ref, g3_ref, wg_ref, wp_ref, gf_ref, o_ref):
    h = h_ref[...] + y_ref[...]
    n3 = (h * _rms_scale(h) * g3_ref[...]).astype(BF16)
    gate = jax.nn.sigmoid(jnp.dot(n3, wg_ref[...], preferred_element_type=F32))
    pe = jnp.dot(p_ref[...].astype(BF16), wp_ref[...], preferred_element_type=F32)
    h = h + gate * pe
    o_ref[...] = h * _rms_scale(h) * gf_ref[...]


def _ple(h, y, p_emb, norm3_g, w_gate, w_ple, final_g, tm):
    t = h.shape[0]
    tm = min(tm, t)
    rowb = pl.BlockSpec((tm, D_MODEL), lambda i: (i, 0))
    vec = pl.BlockSpec((1, D_MODEL), lambda i: (0, 0))
    return pl.pallas_call(
        _ple_kernel,
        grid=(t // tm,),
        in_specs=[rowb, rowb, pl.BlockSpec((tm, PLE_DIM), lambda i: (i, 0)), vec,
                  pl.BlockSpec((D_MODEL, D_MODEL), lambda i: (0, 0)),
                  pl.BlockSpec((PLE_DIM, D_MODEL), lambda i: (0, 0)), vec],
        out_specs=rowb,
        out_shape=jax.ShapeDtypeStruct((t, D_MODEL), F32),
        compiler_params=_params("parallel"),
        name="ple_gate_final_norm",
    )(h, y, p_emb, norm3_g.reshape(1, D_MODEL), w_gate, w_ple, final_g.reshape(1, D_MODEL))


def _top_rows(w, n):
    rows = []
    for k in range(n):
        m = jnp.max(w, axis=0, keepdims=True)
        rows.append(m)
        if k + 1 < n:
            w = jnp.where(w == m, -jnp.inf, w)
    return rows


def _peer_score_kernel(xn_ref, wq_ref, k0_ref, k1_ref, d_ref, ea_ref, b_ref, eb_ref):
    qp = jnp.dot(xn_ref[...], wq_ref[...], preferred_element_type=F32).astype(BF16)
    rank = lax.broadcasted_iota(jnp.int32, (PEER_TOPK, 1), 0)
    n_lead = 4
    for h in range(PEER_HEADS):
        qh = qp[:, h * PEER_KDIM:(h + 1) * PEER_KDIM]
        a = lax.dot_general(k0_ref[h], qh, _NT, preferred_element_type=F32)
        b = lax.dot_general(k1_ref[h], qh, _NT, preferred_element_type=F32)
        ra = _top_rows(a, PEER_TOPK + 1)
        rb = _top_rows(b, PEER_TOPK + 1)
        sa = jnp.concatenate(ra[:PEER_TOPK], axis=0)
        sb = jnp.concatenate(rb[:PEER_TOPK], axis=0)
        sb_head = jnp.concatenate(rb[:PEER_TOPK // 2], axis=0)
        sa_tail = jnp.where(rank < n_lead, -jnp.inf, sa)
        cand = jnp.concatenate([ra[0] + sb] + [ra[i] + sb_head for i in range(1, n_lead)]
                               + [rb[i] + sa_tail for i in range(n_lead - 1)], axis=0)
        top = _top_rows(cand, PEER_TOPK + 1)
        v16 = top[PEER_TOPK - 1]
        v17 = jnp.maximum(top[PEER_TOPK], jnp.maximum(ra[0] + rb[PEER_TOPK], rb[0] + ra[PEER_TOPK]))
        cut = jnp.where(v17 == -jnp.inf, v16, 0.5 * (v16 + v17))
        vmax = ra[0] + rb[0]
        z = jnp.sum(jnp.where(cand >= cut, jnp.exp(cand - vmax), 0.0), axis=0, keepdims=True)
        d_ref[h] = cut - a
        ea_ref[h] = 0.5 * jnp.exp(a - ra[0]) / z
        b_ref[h] = b
        eb_ref[h] = jnp.exp(b - rb[0])


def _peer_scores(xn, wq, k0, k1, tb):
    t = xn.shape[0]
    tb = min(tb, t)
    tab = pl.BlockSpec((PEER_HEADS, N_KEYS, tb), lambda i: (0, 0, i))
    tab_shape = jax.ShapeDtypeStruct((PEER_HEADS, N_KEYS, t), F32)
    full3 = lambda a: pl.BlockSpec(a.shape, lambda i: (0, 0, 0))
    return pl.pallas_call(
        _peer_score_kernel,
        grid=(t // tb,),
        in_specs=[pl.BlockSpec((tb, D_MODEL), lambda i: (i, 0)),
                  pl.BlockSpec(wq.shape, lambda i: (0, 0)), full3(k0), full3(k1)],
        out_specs=[tab, tab, tab, tab],
        out_shape=[tab_shape, tab_shape, tab_shape, tab_shape],
        compiler_params=_params("parallel"),
        name="peer_scores",
    )(xn, wq, k0, k1)


PEER_JQ = 32


def _peer_dense_kernel(xnt_ref, u_ref, vt_ref, d_ref, ea_ref, b_ref, eb_ref, o_ref, at_ref, w_ref, acc_ref, *, tb, eb):
    e = pl.program_id(1)

    @pl.when(e == 0)
    def _():
        acc_ref[...] = jnp.zeros_like(acc_ref)

    at_ref[...] = jnp.dot(u_ref[...], xnt_ref[...], preferred_element_type=F32)

    n_i = eb // N_KEYS

    def gate_rows(jq, carry):
        j0 = pl.multiple_of(jq * PEER_JQ, PEER_JQ)
        for tl in range(tb // LANES):
            ls = slice(tl * LANES, (tl + 1) * LANES)
            gate = [jnp.zeros((PEER_JQ, LANES), F32) for _ in range(n_i)]
            for h in range(PEER_HEADS):
                bq = b_ref[h, pl.ds(j0, PEER_JQ), ls]
                ebq = eb_ref[h, pl.ds(j0, PEER_JQ), ls]
                for ii in range(n_i):
                    hit = bq >= d_ref[h, ii:ii + 1, ls]
                    gate[ii] = jnp.where(hit, gate[ii] + ebq * ea_ref[h, ii:ii + 1, ls], gate[ii])
            for ii in range(n_i):
                r = pl.multiple_of(ii * N_KEYS + j0, PEER_JQ)
                at = at_ref[pl.ds(r, PEER_JQ), ls]
                gelu2 = at * (1.0 + lax.erf(at * math.sqrt(0.5)))
                w_ref[pl.ds(r, PEER_JQ), ls] = (gate[ii] * gelu2).astype(BF16)
        return carry

    lax.fori_loop(0, N_KEYS // PEER_JQ, gate_rows, 0)

    acc_ref[...] += jnp.dot(vt_ref[...], w_ref[...], preferred_element_type=F32)

    @pl.when(e == pl.num_programs(1) - 1)
    def _():
        o_ref[...] = acc_ref[...].T


def _peer_dense(xnt, u, vt, d, ea, b, eb_tab, tb, eb):
    t = xnt.shape[1]
    tb = min(tb, t)
    rows = eb // N_KEYS
    col_tab = pl.BlockSpec((PEER_HEADS, N_KEYS, tb), lambda i, e: (0, 0, i))
    row_tab = pl.BlockSpec((PEER_HEADS, rows, tb), lambda i, e: (0, e, i))
    return pl.pallas_call(
        functools.partial(_peer_dense_kernel, tb=tb, eb=eb),
        grid=(t // tb, u.shape[0] // eb),
        in_specs=[
            pl.BlockSpec((D_MODEL, tb), lambda i, e: (0, i)),
            pl.BlockSpec((eb, D_MODEL), lambda i, e: (e, 0)),
            pl.BlockSpec((D_MODEL, eb), lambda i, e: (0, e)),
            row_tab, row_tab, col_tab, col_tab,
        ],
        out_specs=pl.BlockSpec((tb, D_MODEL), lambda i, e: (i, 0)),
        out_shape=jax.ShapeDtypeStruct((t, D_MODEL), F32),
        scratch_shapes=[
            pltpu.VMEM((eb, tb), F32),
            pltpu.VMEM((eb, tb), BF16),
            pltpu.VMEM((D_MODEL, tb), F32),
        ],
        compiler_params=_params("parallel", "arbitrary"),
        name="peer_dense",
    )(xnt, u, vt, d, ea, b, eb_tab)


def _transpose_cast_kernel(x_ref, o_ref):
    o_ref[...] = x_ref[...].T.astype(BF16)


def _transpose_cast(x, tr, tc):
    r, c = x.shape
    return pl.pallas_call(
        _transpose_cast_kernel,
        grid=(r // tr, c // tc),
        in_specs=[pl.BlockSpec((tr, tc), lambda i, j: (i, j))],
        out_specs=pl.BlockSpec((tc, tr), lambda i, j: (j, i)),
        out_shape=jax.ShapeDtypeStruct((c, r), BF16),
        compiler_params=_params("parallel", "parallel"),
        name="transpose_cast",
    )(x)


def _rope_swap(w):
    half = w.shape[-1] // 2
    return jnp.concatenate([-w[..., half:], w[..., :half]], axis=-1)


def _prepare_weights(w_in, w_uq, w_ukv, peer_keys):
    c_rg = 2 * RET_QK_W + 2 * RET_V_W
    c_ckv = c_rg + Q_LORA
    c_kr = c_ckv + KV_LORA
    c_ga = c_kr + QK_ROPE
    w_big = jnp.concatenate([w_in[:, :c_rg], w_in[:, c_ga:]], axis=1).astype(BF16)
    w_kr = w_in[:, c_kr:c_ga]
    w_small = jnp.concatenate([w_in[:, c_rg:c_kr], w_kr, _rope_swap(w_kr)], axis=1).astype(BF16)
    q_rope = w_uq[:, :, QK_NOPE:]
    wq = jnp.concatenate([w_uq[:, :, :QK_NOPE], q_rope, _rope_swap(q_rope)], axis=-1)
    wq = wq.reshape(Q_LORA, MLA_HEADS * QK_PAD).astype(BF16)
    wk = w_ukv[:, :, :QK_NOPE].reshape(KV_LORA, MLA_HEADS * QK_NOPE).astype(BF16)
    wv = w_ukv[:, :, QK_NOPE:].reshape(KV_LORA, MLA_HEADS * V_HEAD).astype(BF16)
    w_uk_t = jnp.transpose(w_ukv[:, :, :QK_NOPE], (1, 2, 0)).astype(BF16)
    w_uv = jnp.transpose(w_ukv[:, :, QK_NOPE:], (1, 0, 2)).astype(BF16)
    zeros = jnp.zeros((PEER_HEADS, N_KEYS, PEER_HALF), F32)
    k0 = jnp.concatenate([peer_keys[:, 0], zeros], axis=-1).astype(BF16)
    k1 = jnp.concatenate([zeros, peer_keys[:, 1]], axis=-1).astype(BF16)
    return w_big, w_small, wq, wk, wv, w_uk_t, w_uv, k0, k1


def _q_rope_tables(pos):
    cos, sin = _rope_tables(pos, QK_ROPE // 2)
    zeros = jnp.zeros((pos.shape[0], LANES - QK_ROPE), F32)
    return jnp.concatenate([cos, cos, zeros], axis=1), jnp.concatenate([sin, sin, zeros], axis=1)


def kernel(x_prompt, x_sample, cache_ckv, cache_krope, state_ret, page_table, p_prompt, p_sample, norm1_g, w_in, q_norm_g, w_uq, kv_norm_g, w_ukv, w_ret_o, w_mla_o, w_o, norm2_g, peer_wq, peer_keys, peer_u, peer_v, norm3_g, w_ple_gate, w_ple, final_norm_g):
    b_p, l_p, _ = x_prompt.shape
    b_s, l_s, _ = x_sample.shape
    depth = w_in.shape[0]
    assert depth == 1
    past_len = page_table.shape[1] * PAGE_SIZE
    t_p = b_p * l_p
    t_s = b_s * l_s

    w_big, w_small, wq, wk, wv, w_uk_t, w_uv, k0, k1 = _prepare_weights(w_in[0], w_uq[0], w_ukv[0], peer_keys[0])
    w_ret_o_b = w_ret_o[0].astype(BF16)
    w_mla_o_b = w_mla_o[0].astype(BF16)
    w_o_b = w_o[0].astype(BF16)
    peer_wq_b = peer_wq[0].astype(BF16)
    u_b = peer_u[0].astype(BF16)
    vt_b = _transpose_cast(peer_v[0], 1024, 512)
    w_gate_b = w_ple_gate[0].astype(BF16)
    w_ple_b = w_ple[0].astype(BF16)

    xp = x_prompt.reshape(t_p, D_MODEL)
    xs = x_sample.reshape(t_s, D_MODEL)
    pos_p = jnp.arange(l_p, dtype=jnp.int32)
    pos_s = past_len + jnp.arange(l_s, dtype=jnp.int32)

    big_p = _norm_matmul(xp, norm1_g[0], w_big, BF16, TILES["in_proj_rows"], TILES["in_proj_cols"])
    big_s = _norm_matmul(xs, norm1_g[0], w_big, BF16, TILES["in_proj_rows"], TILES["in_proj_cols"])
    small_p = _norm_matmul(xp, norm1_g[0], w_small, F32, TILES["in_proj_rows"], SMALL_W)
    small_s = _norm_matmul(xs, norm1_g[0], w_small, F32, TILES["in_proj_rows"], SMALL_W)

    ro_p, state_p = _retention_prompt(big_p, b_p, l_p, TILES["retention_rows"])
    ro_s, state_s = _retention_sample(big_s, state_ret[0], b_s, l_s, past_len)

    cos_p, sin_p = _q_rope_tables(pos_p)
    cos_p = jnp.tile(cos_p, (b_p, 1))
    sin_p = jnp.tile(sin_p, (b_p, 1))
    q_p, ckv_p, kr_p, k_p, v_p = _mla_proj(small_p, q_norm_g[0], kv_norm_g[0], wq, wk, wv, cos_p, sin_p,
                                           TILES["mla_proj_rows"], True)
    mo_p = _flash_attention(q_p, k_p, v_p, b_p, l_p, TILES["flash_rows"])

    cos_s, sin_s = _q_rope_tables(pos_s)
    cos_s = jnp.tile(cos_s, (b_s, 1))
    sin_s = jnp.tile(sin_s, (b_s, 1))
    q_s, ckv_s, kr_s = _mla_proj(small_s, q_norm_g[0], kv_norm_g[0], wq, wk, wv, cos_s, sin_s,
                                 TILES["mla_proj_rows"], False)
    q_lat = _head_matmul(q_s, w_uk_t, 2, BF16).reshape(b_s, l_s * MLA_HEADS, KV_LORA)
    q_rope = q_s.reshape(t_s, MLA_HEADS, QK_PAD)[:, :, QK_NOPE:QK_NOPE + QK_ROPE].reshape(b_s, l_s * MLA_HEADS, QK_ROPE)
    o_lat = _decode_attention(q_lat, q_rope, cache_ckv, jnp.swapaxes(cache_krope, 2, 3), page_table,
                              ckv_s.reshape(b_s, l_s, KV_LORA), kr_s.reshape(b_s, l_s, QK_ROPE), TILES["decode_pages"])
    mo_s = _head_matmul(o_lat.reshape(t_s, MLA_HEADS * KV_LORA), w_uv, 1, BF16)

    outs = []
    for x, big, ro, mo, p_emb in ((xp, big_p, ro_p, mo_p, p_prompt[0]), (xs, big_s, ro_s, mo_s, p_sample[0])):
        t = x.shape[0]
        merged = _merge(ro, mo, w_ret_o_b, w_mla_o_b, big, TILES["merge_rows"], TILES["merge_cols"])
        h, xn, xnt = _oproj(x, merged, w_o_b, norm2_g[0], TILES["out_proj_rows"])
        d, ea, b, eb_tab = _peer_scores(xn, peer_wq_b, k0, k1, TILES["peer_score_tokens"])
        y = _peer_dense(xnt, u_b, vt_b, d, ea, b, eb_tab, TILES["peer_tokens"], TILES["peer_experts"])
        outs.append(_ple(h, y, p_emb.reshape(t, PLE_DIM), norm3_g[0], w_gate_b, w_ple_b, final_norm_g,
                         TILES["ple_rows"]))

    y_prompt = outs[0].reshape(b_p, l_p, D_MODEL)
    y_sample = outs[1].reshape(b_s, l_s, D_MODEL)
    return (y_prompt, y_sample,
            ckv_p.reshape(1, b_p, l_p, KV_LORA), kr_p.reshape(1, b_p, l_p, QK_ROPE), state_p[None],
            ckv_s.reshape(1, b_s, l_s, KV_LORA), kr_s.reshape(1, b_s, l_s, QK_ROPE), state_s[None])
```

```python
import functools
import math

import jax
import jax.numpy as jnp
from jax import lax
from jax.experimental import pallas as pl
from jax.experimental.pallas import tpu as pltpu

F32 = jnp.float32
BF16 = jnp.bfloat16

D_MODEL = 2048
RET_HEADS = 8
RET_DK = 128
RET_DV = 256
RET_CHUNK = 128
RET_QK_W = RET_HEADS * RET_DK
RET_V_W = RET_HEADS * RET_DV
MLA_HEADS = 16
Q_LORA = 768
KV_LORA = 512
QK_NOPE = 128
QK_ROPE = 64
V_HEAD = 128
MLA_SCALE = (QK_NOPE + QK_ROPE) ** -0.5
ROPE_THETA = 10000.0
PEER_HEADS = 8
N_KEYS = 128
PEER_KDIM = 128
PEER_HALF = PEER_KDIM // 2
PEER_TOPK = 16
PLE_DIM = 256
EPS = 1e-6
PAGE_SIZE = 128

LANES = 128
QK_PAD = 2 * LANES
SMALL_W = Q_LORA + KV_LORA + LANES
BIG_W = 2 * RET_QK_W + 2 * RET_V_W + 2 * D_MODEL
VMEM_LIMIT = 56 * 1024 * 1024
NEG_BIG = -1e30

TILES = dict(
    in_proj_rows=1024, in_proj_cols=2048,
    retention_rows=256,
    mla_proj_rows=256,
    flash_rows=1024,
    decode_pages=16,
    merge_rows=1024, merge_cols=512,
    out_proj_rows=512,
    peer_score_tokens=256,
    peer_tokens=512, peer_experts=1024,
    ple_rows=512,
)

_NT = (((1,), (1,)), ((), ()))
_TN = (((0,), (0,)), ((), ()))
_NN = (((1,), (0,)), ((), ()))


def _params(*sem):
    return pltpu.CompilerParams(dimension_semantics=sem, vmem_limit_bytes=VMEM_LIMIT)


def _rms_scale(x):
    return lax.rsqrt(jnp.mean(x * x, axis=-1, keepdims=True) + EPS)


def _half_swap(x):
    return pltpu.roll(x, LANES // 2, 1)


def _norm_matmul_kernel(x_ref, g_ref, w_ref, o_ref, xn_ref):
    @pl.when(pl.program_id(1) == 0)
    def _():
        x = x_ref[...]
        xn_ref[...] = (x * _rms_scale(x) * g_ref[...]).astype(BF16)

    o_ref[...] = lax.dot_general(xn_ref[...], w_ref[...], _NT, preferred_element_type=F32).astype(o_ref.dtype)


def _norm_matmul(x, g, w, out_dtype, tm, tn):
    t, d = x.shape
    n = w.shape[0]
    tm = min(tm, t)
    return pl.pallas_call(
        _norm_matmul_kernel,
        grid=(t // tm, n // tn),
        in_specs=[
            pl.BlockSpec((tm, d), lambda i, j: (i, 0)),
            pl.BlockSpec((1, d), lambda i, j: (0, 0)),
            pl.BlockSpec((tn, d), lambda i, j: (j, 0)),
        ],
        out_specs=pl.BlockSpec((tm, tn), lambda i, j: (i, j)),
        out_shape=jax.ShapeDtypeStruct((t, n), out_dtype),
        scratch_shapes=[pltpu.VMEM((tm, d), BF16)],
        compiler_params=_params("parallel", "arbitrary"),
        name="norm_matmul",
    )(x, g.reshape(1, d), w)


def _retention_consts(c):
    lg = jnp.log(1.0 - 2.0 ** (-5.0 - jnp.arange(RET_HEADS, dtype=F32)))
    idx = jnp.arange(c, dtype=F32)
    diff = idx[:, None] - idx[None, :]
    causal = diff >= 0
    dmat = jnp.where(causal[None], jnp.exp(lg[:, None, None] * jnp.where(causal, diff, 0.0)[None]), 0.0)
    q_dec = jnp.exp(lg[:, None] * (idx[None, :] + 1.0))
    k_dec = jnp.exp(lg[:, None] * (c - 1.0 - idx[None, :]))
    s_dec = jnp.exp(lg * c)
    qd = jnp.broadcast_to(q_dec[:, :, None], (RET_HEADS, c, RET_DK))
    kd = jnp.broadcast_to(k_dec[:, :, None], (RET_HEADS, c, RET_DK))
    sd = jnp.broadcast_to(s_dec[:, None, None], (RET_HEADS, 8, RET_DV))
    return dmat, qd, kd, sd


def _rope_tables(pos, half):
    inv = ROPE_THETA ** (-jnp.arange(half, dtype=F32) / half)
    ang = pos.astype(F32)[:, None] * inv[None, :]
    return jnp.cos(ang), jnp.sin(ang)


def _ret_head(q, k, v, s, dmat, qd, kd, sd):
    qb = q.astype(BF16)
    kb = k.astype(BF16)
    att = lax.dot_general(qb, kb, _NT, preferred_element_type=F32) * dmat
    o = jnp.dot(att.astype(BF16), v, preferred_element_type=F32)
    o = o + jnp.dot((q * qd).astype(BF16), s.astype(BF16), preferred_element_type=F32)
    s_new = s * sd + lax.dot_general((k * kd).astype(BF16), v, _TN, preferred_element_type=F32)
    return o, s_new


def _ret_prompt_kernel(rq_ref, rk_ref, rv_ref, rg_ref, cos_ref, sin_ref, dmat_ref, qd_ref, kd_ref, sd_ref,
                       o_ref, s_ref, *, n_sub):
    @pl.when(pl.program_id(1) == 0)
    def _():
        s_ref[...] = jnp.zeros_like(s_ref)

    c = RET_CHUNK
    for sub in range(n_sub):
        rows = slice(sub * c, (sub + 1) * c)
        cos = cos_ref[rows, :]
        sin = sin_ref[rows, :]
        for h in range(RET_HEADS):
            qk = slice(h * RET_DK, (h + 1) * RET_DK)
            vv = slice(h * RET_DV, (h + 1) * RET_DV)
            q = rq_ref[rows, qk].astype(F32)
            k = rk_ref[rows, qk].astype(F32)
            q = q * cos + _half_swap(q) * sin
            k = (k * cos + _half_swap(k) * sin) * (RET_DK ** -0.5)
            o, s_new = _ret_head(q, k, rv_ref[rows, vv], s_ref[0, h], dmat_ref[h], qd_ref[h], kd_ref[h],
                                 sd_ref[h, 0:1, :])
            s_ref[0, h] = s_new
            rg = rg_ref[rows, vv].astype(F32)
            o_ref[rows, vv] = (o * _rms_scale(o) * (rg * jax.nn.sigmoid(rg))).astype(BF16)


def _retention_prompt(proj, b, l, rb):
    rb = min(rb, l)
    n_r = l // rb
    cos, sin = _rope_tables(jnp.arange(l, dtype=jnp.int32), RET_DK // 2)
    cos2 = jnp.concatenate([cos, cos], axis=1)
    sin2 = jnp.concatenate([-sin, sin], axis=1)
    dmat, qd, kd, sd = _retention_consts(RET_CHUNK)
    row = lambda bi, r: bi * n_r + r
    const3 = lambda bi, r: (0, 0, 0)
    return pl.pallas_call(
        functools.partial(_ret_prompt_kernel, n_sub=rb // RET_CHUNK),
        grid=(b, n_r),
        in_specs=[
            pl.BlockSpec((rb, RET_QK_W), lambda bi, r: (row(bi, r), 0)),
            pl.BlockSpec((rb, RET_QK_W), lambda bi, r: (row(bi, r), 1)),
            pl.BlockSpec((rb, RET_V_W), lambda bi, r: (row(bi, r), 1)),
            pl.BlockSpec((rb, RET_V_W), lambda bi, r: (row(bi, r), 2)),
            pl.BlockSpec((rb, RET_DK), lambda bi, r: (r, 0)),
            pl.BlockSpec((rb, RET_DK), lambda bi, r: (r, 0)),
            pl.BlockSpec(dmat.shape, const3),
            pl.BlockSpec(qd.shape, const3),
            pl.BlockSpec(kd.shape, const3),
            pl.BlockSpec(sd.shape, const3),
        ],
        out_specs=[
            pl.BlockSpec((rb, RET_V_W), lambda bi, r: (row(bi, r), 0)),
            pl.BlockSpec((1, RET_HEADS, RET_DK, RET_DV), lambda bi, r: (bi, 0, 0, 0)),
        ],
        out_shape=[
            jax.ShapeDtypeStruct((b * l, RET_V_W), BF16),
            jax.ShapeDtypeStruct((b, RET_HEADS, RET_DK, RET_DV), F32),
        ],
        compiler_params=_params("parallel", "arbitrary"),
        name="retention_prompt",
    )(proj, proj, proj, proj, cos2, sin2, dmat, qd, kd, sd)


def _ret_sample_kernel(rq_ref, rk_ref, rv_ref, rg_ref, cos_ref, sin_ref, dmat_ref, qd_ref, kd_ref, sd_ref, s0_ref,
                       o_ref, s_ref, *, l):
    cos = cos_ref[...]
    sin = sin_ref[...]
    first = lax.broadcasted_iota(jnp.int32, (2 * l, 1), 0) < l
    for h in range(RET_HEADS):
        qk = slice(h * RET_DK, (h + 1) * RET_DK)
        vv = slice(h * RET_DV, (h + 1) * RET_DV)
        q = rq_ref[:, qk].astype(F32)
        k = rk_ref[:, qk].astype(F32)
        q = q * cos + _half_swap(q) * sin
        k = (k * cos + _half_swap(k) * sin) * (RET_DK ** -0.5)
        v = rv_ref[:, vv]
        att = lax.dot_general(q.astype(BF16), k.astype(BF16), _NT, preferred_element_type=F32) * dmat_ref[h]
        o = jnp.dot(att.astype(BF16), v, preferred_element_type=F32)
        qdb = (q * qd_ref[h]).astype(BF16)
        kdk = k * kd_ref[h]
        sd = sd_ref[h, 0:1, :]
        s_a = s0_ref[0, h]
        s_b = s0_ref[1, h]
        o = o + jnp.where(first,
                          jnp.dot(qdb, s_a.astype(BF16), preferred_element_type=F32),
                          jnp.dot(qdb, s_b.astype(BF16), preferred_element_type=F32))
        k_a = jnp.where(first, kdk, 0.0).astype(BF16)
        k_b = jnp.where(first, 0.0, kdk).astype(BF16)
        s_ref[0, h] = s_a * sd + lax.dot_general(k_a, v, _TN, preferred_element_type=F32)
        s_ref[1, h] = s_b * sd + lax.dot_general(k_b, v, _TN, preferred_element_type=F32)
        rg = rg_ref[:, vv].astype(F32)
        o_ref[:, vv] = (o * _rms_scale(o) * (rg * jax.nn.sigmoid(rg))).astype(BF16)


def _retention_sample(proj, state, b, l, pos0):
    assert b % 2 == 0
    cos, sin = _rope_tables(pos0 + jnp.arange(l, dtype=jnp.int32), RET_DK // 2)
    cos2 = jnp.tile(jnp.concatenate([cos, cos], axis=1), (2, 1))
    sin2 = jnp.tile(jnp.concatenate([-sin, sin], axis=1), (2, 1))
    dmat, qd, kd, sd = _retention_consts(l)
    zeros = jnp.zeros_like(dmat)
    dmat2 = jnp.concatenate([jnp.concatenate([dmat, zeros], axis=2), jnp.concatenate([zeros, dmat], axis=2)], axis=1)
    qd2 = jnp.tile(qd, (1, 2, 1))
    kd2 = jnp.tile(kd, (1, 2, 1))
    rows = 2 * l
    const2 = lambda i: (0, 0)
    const3 = lambda i: (0, 0, 0)
    return pl.pallas_call(
        functools.partial(_ret_sample_kernel, l=l),
        grid=(b // 2,),
        in_specs=[
            pl.BlockSpec((rows, RET_QK_W), lambda i: (i, 0)),
            pl.BlockSpec((rows, RET_QK_W), lambda i: (i, 1)),
            pl.BlockSpec((rows, RET_V_W), lambda i: (i, 1)),
            pl.BlockSpec((rows, RET_V_W), lambda i: (i, 2)),
            pl.BlockSpec((rows, RET_DK), const2),
            pl.BlockSpec((rows, RET_DK), const2),
            pl.BlockSpec(dmat2.shape, const3),
            pl.BlockSpec(qd2.shape, const3),
            pl.BlockSpec(kd2.shape, const3),
            pl.BlockSpec(sd.shape, const3),
            pl.BlockSpec((2, RET_HEADS, RET_DK, RET_DV), lambda i: (i, 0, 0, 0)),
        ],
        out_specs=[
            pl.BlockSpec((rows, RET_V_W), lambda i: (i, 0)),
            pl.BlockSpec((2, RET_HEADS, RET_DK, RET_DV), lambda i: (i, 0, 0, 0)),
        ],
        out_shape=[
            jax.ShapeDtypeStruct((b * l, RET_V_W), BF16),
            jax.ShapeDtypeStruct((b, RET_HEADS, RET_DK, RET_DV), F32),
        ],
        compiler_params=_params("parallel"),
        name="retention_sample",
    )(proj, proj, proj, proj, cos2, sin2, dmat2, qd2, kd2, sd, state)


def _mla_proj_kernel(x_ref, qg_ref, kvg_ref, wq_ref, wk_ref, wv_ref, cos_ref, sin_ref,
                     q_ref, ckv_ref, kr_ref, *kv_refs):
    cos = cos_ref[...]
    sin = sin_ref[...]
    cq = x_ref[:, 0:Q_LORA]
    cqn = (cq * _rms_scale(cq) * qg_ref[...]).astype(BF16)
    q = jnp.dot(cqn, wq_ref[...], preferred_element_type=F32)
    for h in range(MLA_HEADS):
        lo = h * QK_PAD
        q_ref[:, lo:lo + LANES] = q[:, lo:lo + LANES].astype(BF16)
        r = q[:, lo + LANES:lo + QK_PAD]
        q_ref[:, lo + LANES:lo + QK_PAD] = (r * cos + _half_swap(r) * sin).astype(BF16)
    ckv = x_ref[:, Q_LORA:Q_LORA + KV_LORA]
    ckvn = ckv * _rms_scale(ckv) * kvg_ref[...]
    ckv_ref[...] = ckvn
    krx = x_ref[:, Q_LORA + KV_LORA:SMALL_W]
    krr = krx * cos + _half_swap(krx) * sin
    kr_ref[...] = krr[:, 0:QK_ROPE]
    if kv_refs:
        k_ref, v_ref = kv_refs
        cb = ckvn.astype(BF16)
        kn = jnp.dot(cb, wk_ref[...], preferred_element_type=F32)
        krb = krr.astype(BF16)
        for h in range(MLA_HEADS):
            lo = h * QK_PAD
            k_ref[:, lo:lo + LANES] = kn[:, h * QK_NOPE:(h + 1) * QK_NOPE].astype(BF16)
            k_ref[:, lo + LANES:lo + QK_PAD] = krb
        v_ref[...] = jnp.dot(cb, wv_ref[...], preferred_element_type=F32).astype(BF16)


def _mla_proj(small, q_norm_g, kv_norm_g, wq, wk, wv, cosq, sinq, tm, with_kv):
    t = small.shape[0]
    tm = min(tm, t)
    rowb = lambda n: pl.BlockSpec((tm, n), lambda i: (i, 0))
    full = lambda a: pl.BlockSpec(a.shape, lambda i: (0, 0))
    qg = q_norm_g.reshape(1, Q_LORA)
    kvg = kv_norm_g.reshape(1, KV_LORA)
    out_specs = [rowb(MLA_HEADS * QK_PAD), rowb(KV_LORA), rowb(QK_ROPE)]
    out_shape = [
        jax.ShapeDtypeStruct((t, MLA_HEADS * QK_PAD), BF16),
        jax.ShapeDtypeStruct((t, KV_LORA), F32),
        jax.ShapeDtypeStruct((t, QK_ROPE), F32),
    ]
    if with_kv:
        out_specs += [rowb(MLA_HEADS * QK_PAD), rowb(MLA_HEADS * V_HEAD)]
        out_shape += [
            jax.ShapeDtypeStruct((t, MLA_HEADS * QK_PAD), BF16),
            jax.ShapeDtypeStruct((t, MLA_HEADS * V_HEAD), BF16),
        ]
    return pl.pallas_call(
        _mla_proj_kernel,
        grid=(t // tm,),
        in_specs=[rowb(SMALL_W), full(qg), full(kvg), full(wq), full(wk), full(wv), rowb(LANES), rowb(LANES)],
        out_specs=out_specs,
        out_shape=out_shape,
        compiler_params=_params("parallel"),
        name="mla_proj",
    )(small, qg, kvg, wq, wk, wv, cosq, sinq)


FLASH_HEADS = 2


def _flash_kernel(q_ref, k_ref, v_ref, o_ref, *, tq):
    qi = pl.program_id(2)
    qs = [q_ref[:, g * QK_PAD:(g + 1) * QK_PAD] for g in range(FLASH_HEADS)]

    def scores(q, off, n, g):
        k = k_ref[pl.ds(off, n), g * QK_PAD:(g + 1) * QK_PAD]
        v = v_ref[pl.ds(off, n), g * V_HEAD:(g + 1) * V_HEAD]
        s = lax.dot_general(q, k, _NT, preferred_element_type=F32) * (MLA_SCALE * math.log2(math.e))
        return s, v

    def update(carry, s, v):
        m, l, acc = carry
        m_new = jnp.maximum(m, jnp.max(s, axis=-1, keepdims=True))
        corr = jnp.exp2(m - m_new)
        p = jnp.exp2(s - m_new)
        l = l * corr + jnp.sum(p, axis=-1, keepdims=True)
        acc = acc * corr + jnp.dot(p.astype(BF16), v, preferred_element_type=F32)
        return m_new, l, acc

    def body(j, carries):
        off = pl.multiple_of(j * tq, tq)
        return tuple(update(carries[g], *scores(qs[g], off, tq, g)) for g in range(FLASH_HEADS))

    init = (jnp.full((tq, 1), NEG_BIG, F32), jnp.zeros((tq, 1), F32), jnp.zeros((tq, V_HEAD), F32))
    carries = lax.fori_loop(0, qi, body, (init,) * FLASH_HEADS)

    diag = pl.multiple_of(qi * tq, tq)
    row = lax.broadcasted_iota(jnp.int32, (tq, tq), 0)
    col = lax.broadcasted_iota(jnp.int32, (tq, tq), 1)
    for g in range(FLASH_HEADS):
        s, v = scores(qs[g], diag, tq, g)
        _, l, acc = update(carries[g], jnp.where(col <= row, s, -jnp.inf), v)
        o_ref[:, g * V_HEAD:(g + 1) * V_HEAD] = (acc / l).astype(BF16)


def _flash_attention(q, k, v, b, l, tq):
    tq = min(tq, l)
    n_q = l // tq
    qw = FLASH_HEADS * QK_PAD
    vw = FLASH_HEADS * V_HEAD
    return pl.pallas_call(
        functools.partial(_flash_kernel, tq=tq),
        grid=(b, MLA_HEADS // FLASH_HEADS, n_q),
        in_specs=[
            pl.BlockSpec((tq, qw), lambda bi, h, i: (bi * n_q + i, h)),
            pl.BlockSpec((l, qw), lambda bi, h, i: (bi, h)),
            pl.BlockSpec((l, vw), lambda bi, h, i: (bi, h)),
        ],
        out_specs=pl.BlockSpec((tq, vw), lambda bi, h, i: (bi * n_q + i, h)),
        out_shape=jax.ShapeDtypeStruct((b * l, MLA_HEADS * V_HEAD), BF16),
        compiler_params=_params("parallel", "parallel", "arbitrary"),
        name="mla_prompt_attention",
    )(q, k, v)


def _head_matmul_kernel(x_ref, w_ref, o_ref):
    o_ref[...] = jnp.dot(x_ref[...], w_ref[0], preferred_element_type=F32).astype(o_ref.dtype)


def _head_matmul(x, w, col_stride, out_dtype):
    m = x.shape[0]
    heads, kd, n = w.shape
    return pl.pallas_call(
        _head_matmul_kernel,
        grid=(heads,),
        in_specs=[
            pl.BlockSpec((m, kd), lambda h: (0, h * col_stride)),
            pl.BlockSpec((1, kd, n), lambda h: (h, 0, 0)),
        ],
        out_specs=pl.BlockSpec((m, n), lambda h: (0, h)),
        out_shape=jax.ShapeDtypeStruct((m, heads * n), out_dtype),
        compiler_params=_params("parallel"),
        name="head_matmul",
    )(x, w)


def _decode_kernel(pt_ref, ql_ref, qr_ref, ckv_hbm, krt_hbm, cn_ref, kn_ref, o_ref,
                   m_ref, l_ref, acc_ref, kc_ref, krc_ref, kbuf, krbuf, sem, *, pg, l_new):
    b = pl.program_id(0)
    g = pl.program_id(1)
    n_b = pl.num_programs(0)
    n_g = pl.num_programs(1)
    step = b * n_g + g
    slot = step % 2

    def page_copies(bi, gi, sl):
        copies = []
        for i in range(pg):
            page = pt_ref[bi, gi * pg + i]
            rs = pl.ds(i * PAGE_SIZE, PAGE_SIZE)
            copies.append(pltpu.make_async_copy(ckv_hbm.at[0, page], kbuf.at[sl, rs, :], sem.at[sl]))
            copies.append(pltpu.make_async_copy(krt_hbm.at[0, page], krbuf.at[sl, i], sem.at[sl]))
        return copies

    def start_all(copies):
        for k, c in enumerate(copies):
            c.start(priority=k // 2 % 2)

    @pl.when(step == 0)
    def _():
        start_all(page_copies(b, g, slot))

    last_g = g == n_g - 1
    nb = jnp.where(last_g, b + 1, b)
    ng = jnp.where(last_g, 0, g + 1)

    @pl.when(step + 1 < n_b * n_g)
    def _():
        start_all(page_copies(nb, ng, 1 - slot))

    for c in page_copies(b, g, slot):
        c.wait()

    @pl.when(g == 0)
    def _():
        m_ref[...] = jnp.full_like(m_ref, NEG_BIG)
        l_ref[...] = jnp.zeros_like(l_ref)
        acc_ref[...] = jnp.zeros_like(acc_ref)

    ql = ql_ref[0]
    qr = qr_ref[0]

    def absorb(s, vals):
        m = m_ref[...]
        m_new = jnp.maximum(m, jnp.max(s, axis=-1, keepdims=True))
        corr = jnp.exp(m - m_new)
        p = jnp.exp(s - m_new[:, 0:1])
        l_ref[...] = l_ref[...] * corr + jnp.sum(p, axis=-1, keepdims=True)
        pv = acc_ref[...] * corr[:, 0:1]
        for i, val in enumerate(vals):
            w = val.shape[0]
            pv = pv + jnp.dot(p[:, i * w:(i + 1) * w].astype(BF16), val, preferred_element_type=F32)
        acc_ref[...] = pv
        m_ref[...] = m_new

    def score(ckv, kr, kr_dims):
        return (lax.dot_general(ql, ckv, _NT, preferred_element_type=F32)
                + lax.dot_general(qr, kr, kr_dims, preferred_element_type=F32)) * MLA_SCALE

    kc_ref[...] = kbuf[slot].astype(BF16)
    for i in range(pg):
        krc_ref[:, i * PAGE_SIZE:(i + 1) * PAGE_SIZE] = krbuf[slot, i].astype(BF16)
    kc = kc_ref[...]
    absorb(score(kc, krc_ref[...], _NN), [kc])

    @pl.when(g == pl.num_programs(1) - 1)
    def _():
        cn = cn_ref[0].astype(BF16)
        s_new = score(cn, kn_ref[0].astype(BF16), _NT)
        tok = lax.broadcasted_iota(jnp.int32, s_new.shape, 0) // MLA_HEADS
        col = lax.broadcasted_iota(jnp.int32, s_new.shape, 1)
        absorb(jnp.where((col <= tok) & (col < l_new), s_new, -jnp.inf), [cn])
        o_ref[0] = (acc_ref[...] / l_ref[:, 0:1]).astype(BF16)


def _decode_attention(q_lat, q_rope, cache_ckv, cache_krope_t, page_table, ckv_new, kr_new, pg):
    b, rows, _ = q_lat.shape
    n_pages = page_table.shape[1]
    pg = min(pg, n_pages)
    l_new = ckv_new.shape[1]
    pad = 16 - l_new
    cn = jnp.pad(ckv_new, ((0, 0), (0, pad), (0, 0)))
    kn = jnp.pad(kr_new, ((0, 0), (0, pad), (0, 0)))
    per_b = lambda n: pl.BlockSpec((1, n[0], n[1]), lambda bi, g, pt: (bi, 0, 0))
    hbm = pl.BlockSpec(memory_space=pl.ANY)
    keys = pg * PAGE_SIZE
    grid_spec = pltpu.PrefetchScalarGridSpec(
        num_scalar_prefetch=1,
        grid=(b, n_pages // pg),
        in_specs=[per_b((rows, KV_LORA)), per_b((rows, QK_ROPE)), hbm, hbm,
                  per_b((16, KV_LORA)), per_b((16, QK_ROPE))],
        out_specs=per_b((rows, KV_LORA)),
        scratch_shapes=[
            pltpu.VMEM((rows, LANES), F32),
            pltpu.VMEM((rows, LANES), F32),
            pltpu.VMEM((rows, KV_LORA), F32),
            pltpu.VMEM((keys, KV_LORA), BF16),
            pltpu.VMEM((QK_ROPE, keys), BF16),
            pltpu.VMEM((2, keys, KV_LORA), F32),
            pltpu.VMEM((2, pg, QK_ROPE, PAGE_SIZE), F32),
            pltpu.SemaphoreType.DMA((2,)),
        ],
    )
    return pl.pallas_call(
        functools.partial(_decode_kernel, pg=pg, l_new=l_new),
        grid_spec=grid_spec,
        out_shape=jax.ShapeDtypeStruct((b, rows, KV_LORA), BF16),
        compiler_params=_params("arbitrary", "arbitrary"),
        name="mla_decode_attention",
    )(page_table, q_lat, q_rope, cache_ckv, cache_krope_t, cn, kn)


def _merge_kernel(ro_ref, mo_ref, wr_ref, wm_ref, ga_ref, gb_ref, o_ref):
    t1 = jnp.dot(ro_ref[...], wr_ref[...], preferred_element_type=F32)
    t2 = jnp.dot(mo_ref[...], wm_ref[...], preferred_element_type=F32)
    ga = jax.nn.sigmoid(ga_ref[...].astype(F32))
    gb = jax.nn.sigmoid(gb_ref[...].astype(F32))
    o_ref[...] = (ga * t1 + gb * t2).astype(BF16)


def _merge(ro, mo, w_ret_o, w_mla_o, proj, tm, tn):
    t = ro.shape[0]
    tm = min(tm, t)
    ga0 = (2 * RET_QK_W + 2 * RET_V_W) // tn
    gb0 = ga0 + D_MODEL // tn
    return pl.pallas_call(
        _merge_kernel,
        grid=(t // tm, D_MODEL // tn),
        in_specs=[
            pl.BlockSpec((tm, RET_V_W), lambda i, j: (i, 0)),
            pl.BlockSpec((tm, MLA_HEADS * V_HEAD), lambda i, j: (i, 0)),
            pl.BlockSpec((RET_V_W, tn), lambda i, j: (0, j)),
            pl.BlockSpec((MLA_HEADS * V_HEAD, tn), lambda i, j: (0, j)),
            pl.BlockSpec((tm, tn), lambda i, j: (i, ga0 + j)),
            pl.BlockSpec((tm, tn), lambda i, j: (i, gb0 + j)),
        ],
        out_specs=pl.BlockSpec((tm, tn), lambda i, j: (i, j)),
        out_shape=jax.ShapeDtypeStruct((t, D_MODEL), BF16),
        compiler_params=_params("parallel", "arbitrary"),
        name="branch_merge",
    )(ro, mo, w_ret_o, w_mla_o, proj, proj)


def _oproj_kernel(x_ref, m_ref, w_ref, g_ref, h_ref, xn_ref, xnt_ref):
    h = x_ref[...] + jnp.dot(m_ref[...], w_ref[...], preferred_element_type=F32)
    h_ref[...] = h
    xn = h * _rms_scale(h) * g_ref[...]
    xn_ref[...] = xn.astype(BF16)
    xnt_ref[...] = xn.T.astype(BF16)


def _oproj(x, merged, w_o, norm2_g, tm):
    t = x.shape[0]
    tm = min(tm, t)
    rowb = pl.BlockSpec((tm, D_MODEL), lambda i: (i, 0))
    return pl.pallas_call(
        _oproj_kernel,
        grid=(t // tm,),
        in_specs=[rowb, rowb, pl.BlockSpec((D_MODEL, D_MODEL), lambda i: (0, 0)),
                  pl.BlockSpec((1, D_MODEL), lambda i: (0, 0))],
        out_specs=[rowb, rowb, pl.BlockSpec((D_MODEL, tm), lambda i: (0, i))],
        out_shape=[jax.ShapeDtypeStruct((t, D_MODEL), F32), jax.ShapeDtypeStruct((t, D_MODEL), BF16),
                   jax.ShapeDtypeStruct((D_MODEL, t), BF16)],
        compiler_params=_params("parallel"),
        name="out_proj_residual",
    )(x, merged, w_o, norm2_g.reshape(1, D_MODEL))


def _ple_kernel(h_ref, y_ref, p_ref, g3_ref, wg_ref, wp_ref, gf_ref, o_ref):
    h = h_ref[...] + y_ref[...]
    n3 = (h * _rms_scale(h) * g3_ref[...]).astype(BF16)
    gate = jax.nn.sigmoid(jnp.dot(n3, wg_ref[...], preferred_element_type=F32))
    pe = jnp.dot(p_ref[...].astype(BF16), wp_ref[...], preferred_element_type=F32)
    h = h + gate * pe
    o_ref[...] = h * _rms_scale(h) * gf_ref[...]


def _ple(h, y, p_emb, norm3_g, w_gate, w_ple, final_g, tm):
    t = h.shape[0]
    tm = min(tm, t)
    rowb = pl.BlockSpec((tm, D_MODEL), lambda i: (i, 0))
    vec = pl.BlockSpec((1, D_MODEL), lambda i: (0, 0))
    return pl.pallas_call(
        _ple_kernel,
        grid=(t // tm,),
        in_specs=[rowb, rowb, pl.BlockSpec((tm, PLE_DIM), lambda i: (i, 0)), vec,
                  pl.BlockSpec((D_MODEL, D_MODEL), lambda i: (0, 0)),
                  pl.BlockSpec((PLE_DIM, D_MODEL), lambda i: (0, 0)), vec],
        out_specs=rowb,
        out_shape=jax.ShapeDtypeStruct((t, D_MODEL), F32),
        compiler_params=_params("parallel"),
        name="ple_gate_final_norm",
    )(h, y, p_emb, norm3_g.reshape(1, D_MODEL), w_gate, w_ple, final_g.reshape(1, D_MODEL))


def _top_rows(w, n):
    rows = []
    for k in range(n):
        m = jnp.max(w, axis=0, keepdims=True)
        rows.append(m)
        if k + 1 < n:
            w = jnp.where(w == m, -jnp.inf, w)
    return rows


def _peer_score_kernel(xn_ref, wq_ref, k0_ref, k1_ref, d_ref, ea_ref, b_ref, eb_ref):
    qp = jnp.dot(xn_ref[...], wq_ref[...], preferred_element_type=F32).astype(BF16)
    rank = lax.broadcasted_iota(jnp.int32, (PEER_TOPK, 1), 0)
    n_lead = 4
    for h in range(PEER_HEADS):
        qh = qp[:, h * PEER_KDIM:(h + 1) * PEER_KDIM]
        a = lax.dot_general(k0_ref[h], qh, _NT, preferred_element_type=F32)
        b = lax.dot_general(k1_ref[h], qh, _NT, preferred_element_type=F32)
        ra = _top_rows(a, PEER_TOPK + 1)
        rb = _top_rows(b, PEER_TOPK + 1)
        sa = jnp.concatenate(ra[:PEER_TOPK], axis=0)
        sb = jnp.concatenate(rb[:PEER_TOPK], axis=0)
        sb_head = jnp.concatenate(rb[:PEER_TOPK // 2], axis=0)
        sa_tail = jnp.where(rank < n_lead, -jnp.inf, sa)
        cand = jnp.concatenate([ra[0] + sb] + [ra[i] + sb_head for i in range(1, n_lead)]
                               + [rb[i] + sa_tail for i in range(n_lead - 1)], axis=0)
        top = _top_rows(cand, PEER_TOPK + 1)
        v16 = top[PEER_TOPK - 1]
        v17 = jnp.maximum(top[PEER_TOPK], jnp.maximum(ra[0] + rb[PEER_TOPK], rb[0] + ra[PEER_TOPK]))
        cut = jnp.where(v17 == -jnp.inf, v16, 0.5 * (v16 + v17))
        vmax = ra[0] + rb[0]
        z = jnp.sum(jnp.where(cand >= cut, jnp.exp(cand - vmax), 0.0), axis=0, keepdims=True)
        d_ref[h] = cut - a
        ea_ref[h] = 0.5 * jnp.exp(a - ra[0]) / z
        b_ref[h] = b
        eb_ref[h] = jnp.exp(b - rb[0])


def _peer_scores(xn, wq, k0, k1, tb):
    t = xn.shape[0]
    tb = min(tb, t)
    tab = pl.BlockSpec((PEER_HEADS, N_KEYS, tb), lambda i: (0, 0, i))
    tab_shape = jax.ShapeDtypeStruct((PEER_HEADS, N_KEYS, t), F32)
    full3 = lambda a: pl.BlockSpec(a.shape, lambda i: (0, 0, 0))
    return pl.pallas_call(
        _peer_score_kernel,
        grid=(t // tb,),
        in_specs=[pl.BlockSpec((tb, D_MODEL), lambda i: (i, 0)),
                  pl.BlockSpec(wq.shape, lambda i: (0, 0)), full3(k0), full3(k1)],
        out_specs=[tab, tab, tab, tab],
        out_shape=[tab_shape, tab_shape, tab_shape, tab_shape],
        compiler_params=_params("parallel"),
        name="peer_scores",
    )(xn, wq, k0, k1)


PEER_JQ = 32


def _peer_dense_kernel(xnt_ref, u_ref, vt_ref, d_ref, ea_ref, b_ref, eb_ref, o_ref, at_ref, w_ref, acc_ref, *, tb, eb):
    e = pl.program_id(1)

    @pl.when(e == 0)
    def _():
        acc_ref[...] = jnp.zeros_like(acc_ref)

    at_ref[...] = jnp.dot(u_ref[...], xnt_ref[...], preferred_element_type=F32)

    n_i = eb // N_KEYS

    def gate_rows(jq, carry):
        j0 = pl.multiple_of(jq * PEER_JQ, PEER_JQ)
        for tl in range(tb // LANES):
            ls = slice(tl * LANES, (tl + 1) * LANES)
            gate = [jnp.zeros((PEER_JQ, LANES), F32) for _ in range(n_i)]
            for h in range(PEER_HEADS):
                bq = b_ref[h, pl.ds(j0, PEER_JQ), ls]
                ebq = eb_ref[h, pl.ds(j0, PEER_JQ), ls]
                for ii in range(n_i):
                    hit = bq >= d_ref[h, ii:ii + 1, ls]
                    gate[ii] = jnp.where(hit, gate[ii] + ebq * ea_ref[h, ii:ii + 1, ls], gate[ii])
            for ii in range(n_i):
                r = pl.multiple_of(ii * N_KEYS + j0, PEER_JQ)
                at = at_ref[pl.ds(r, PEER_JQ), ls]
                gelu2 = at * (1.0 + lax.erf(at * math.sqrt(0.5)))
                w_ref[pl.ds(r, PEER_JQ), ls] = (gate[ii] * gelu2).astype(BF16)
        return carry

    lax.fori_loop(0, N_KEYS // PEER_JQ, gate_rows, 0)

    acc_ref[...] += jnp.dot(vt_ref[...], w_ref[...], preferred_element_type=F32)

    @pl.when(e == pl.num_programs(1) - 1)
    def _():
        o_ref[...] = acc_ref[...].T


def _peer_dense(xnt, u, vt, d, ea, b, eb_tab, tb, eb):
    t = xnt.shape[1]
    tb = min(tb, t)
    rows = eb // N_KEYS
    col_tab = pl.BlockSpec((PEER_HEADS, N_KEYS, tb), lambda i, e: (0, 0, i))
    row_tab = pl.BlockSpec((PEER_HEADS, rows, tb), lambda i, e: (0, e, i))
    return pl.pallas_call(
        functools.partial(_peer_dense_kernel, tb=tb, eb=eb),
        grid=(t // tb, u.shape[0] // eb),
        in_specs=[
            pl.BlockSpec((D_MODEL, tb), lambda i, e: (0, i)),
            pl.BlockSpec((eb, D_MODEL), lambda i, e: (e, 0)),
            pl.BlockSpec((D_MODEL, eb), lambda i, e: (0, e)),
            row_tab, row_tab, col_tab, col_tab,
        ],
        out_specs=pl.BlockSpec((tb, D_MODEL), lambda i, e: (i, 0)),
        out_shape=jax.ShapeDtypeStruct((t, D_MODEL), F32),
        scratch_shapes=[
            pltpu.VMEM((eb, tb), F32),
            pltpu.VMEM((eb, tb), BF16),
            pltpu.VMEM((D_MODEL, tb), F32),
        ],
        compiler_params=_params("parallel", "arbitrary"),
        name="peer_dense",
    )(xnt, u, vt, d, ea, b, eb_tab)


def _transpose_cast_kernel(x_ref, o_ref):
    o_ref[...] = x_ref[...].T.astype(BF16)


def _transpose_cast(x, tr, tc):
    r, c = x.shape
    return pl.pallas_call(
        _transpose_cast_kernel,
        grid=(r // tr, c // tc),
        in_specs=[pl.BlockSpec((tr, tc), lambda i, j: (i, j))],
        out_specs=pl.BlockSpec((tc, tr), lambda i, j: (j, i)),
        out_shape=jax.ShapeDtypeStruct((c, r), BF16),
        compiler_params=_params("parallel", "parallel"),
        name="transpose_cast",
    )(x)


def _rope_swap(w):
    half = w.shape[-1] // 2
    return jnp.concatenate([-w[..., half:], w[..., :half]], axis=-1)


def _prepare_weights(w_in, w_uq, w_ukv, peer_keys):
    c_rg = 2 * RET_QK_W + 2 * RET_V_W
    c_ckv = c_rg + Q_LORA
    c_kr = c_ckv + KV_LORA
    c_ga = c_kr + QK_ROPE
    w_in_t = w_in.T
    w_big = jnp.concatenate([w_in_t[:c_rg], w_in_t[c_ga:]], axis=0).astype(BF16)
    w_kr = w_in_t[c_kr:c_ga]
    w_kr_swap = _rope_swap(w_kr.T).T
    w_small = jnp.concatenate([w_in_t[c_rg:c_kr], w_kr, w_kr_swap], axis=0).astype(BF16)
    q_rope = w_uq[:, :, QK_NOPE:]
    wq = jnp.concatenate([w_uq[:, :, :QK_NOPE], q_rope, _rope_swap(q_rope)], axis=-1)
    wq = wq.reshape(Q_LORA, MLA_HEADS * QK_PAD).astype(BF16)
    wk = w_ukv[:, :, :QK_NOPE].reshape(KV_LORA, MLA_HEADS * QK_NOPE).astype(BF16)
    wv = w_ukv[:, :, QK_NOPE:].reshape(KV_LORA, MLA_HEADS * V_HEAD).astype(BF16)
    w_uk_t = jnp.transpose(w_ukv[:, :, :QK_NOPE], (1, 2, 0)).astype(BF16)
    w_uv = jnp.transpose(w_ukv[:, :, QK_NOPE:], (1, 0, 2)).astype(BF16)
    zeros = jnp.zeros((PEER_HEADS, N_KEYS, PEER_HALF), F32)
    k0 = jnp.concatenate([peer_keys[:, 0], zeros], axis=-1).astype(BF16)
    k1 = jnp.concatenate([zeros, peer_keys[:, 1]], axis=-1).astype(BF16)
    return w_big, w_small, wq, wk, wv, w_uk_t, w_uv, k0, k1


def _q_rope_tables(pos):
    cos, sin = _rope_tables(pos, QK_ROPE // 2)
    zeros = jnp.zeros((pos.shape[0], LANES - QK_ROPE), F32)
    return jnp.concatenate([cos, cos, zeros], axis=1), jnp.concatenate([sin, sin, zeros], axis=1)


def kernel(x_prompt, x_sample, cache_ckv, cache_krope, state_ret, page_table, p_prompt, p_sample, norm1_g, w_in, q_norm_g, w_uq, kv_norm_g, w_ukv, w_ret_o, w_mla_o, w_o, norm2_g, peer_wq, peer_keys, peer_u, peer_v, norm3_g, w_ple_gate, w_ple, final_norm_g):
    b_p, l_p, _ = x_prompt.shape
    b_s, l_s, _ = x_sample.shape
    depth = w_in.shape[0]
    assert depth == 1
    past_len = page_table.shape[1] * PAGE_SIZE
    t_p = b_p * l_p
    t_s = b_s * l_s

    w_big, w_small, wq, wk, wv, w_uk_t, w_uv, k0, k1 = _prepare_weights(w_in[0], w_uq[0], w_ukv[0], peer_keys[0])
    w_ret_o_b = w_ret_o[0].astype(BF16)
    w_mla_o_b = w_mla_o[0].astype(BF16)
    w_o_b = w_o[0].astype(BF16)
    peer_wq_b = peer_wq[0].astype(BF16)
    u_b = peer_u[0].astype(BF16)
    vt_b = _transpose_cast(peer_v[0], 1024, 512)
    w_gate_b = w_ple_gate[0].astype(BF16)
    w_ple_b = w_ple[0].astype(BF16)

    xp = x_prompt.reshape(t_p, D_MODEL)
    xs = x_sample.reshape(t_s, D_MODEL)
    pos_p = jnp.arange(l_p, dtype=jnp.int32)
    pos_s = past_len + jnp.arange(l_s, dtype=jnp.int32)

    big_p = _norm_matmul(xp, norm1_g[0], w_big, BF16, TILES["in_proj_rows"], TILES["in_proj_cols"])
    big_s = _norm_matmul(xs, norm1_g[0], w_big, BF16, TILES["in_proj_rows"], TILES["in_proj_cols"])
    small_p = _norm_matmul(xp, norm1_g[0], w_small, F32, TILES["in_proj_rows"], SMALL_W)
    small_s = _norm_matmul(xs, norm1_g[0], w_small, F32, TILES["in_proj_rows"], SMALL_W)

    ro_p, state_p = _retention_prompt(big_p, b_p, l_p, TILES["retention_rows"])
    ro_s, state_s = _retention_sample(big_s, state_ret[0], b_s, l_s, past_len)

    cos_p, sin_p = _q_rope_tables(pos_p)
    cos_p = jnp.tile(cos_p, (b_p, 1))
    sin_p = jnp.tile(sin_p, (b_p, 1))
    q_p, ckv_p, kr_p, k_p, v_p = _mla_proj(small_p, q_norm_g[0], kv_norm_g[0], wq, wk, wv, cos_p, sin_p,
                                           TILES["mla_proj_rows"], True)
    mo_p = _flash_attention(q_p, k_p, v_p, b_p, l_p, TILES["flash_rows"])

    cos_s, sin_s = _q_rope_tables(pos_s)
    cos_s = jnp.tile(cos_s, (b_s, 1))
    sin_s = jnp.tile(sin_s, (b_s, 1))
    q_s, ckv_s, kr_s = _mla_proj(small_s, q_norm_g[0], kv_norm_g[0], wq, wk, wv, cos_s, sin_s,
                                 TILES["mla_proj_rows"], False)
    q_lat = _head_matmul(q_s, w_uk_t, 2, BF16).reshape(b_s, l_s * MLA_HEADS, KV_LORA)
    q_rope = q_s.reshape(t_s, MLA_HEADS, QK_PAD)[:, :, QK_NOPE:QK_NOPE + QK_ROPE].reshape(b_s, l_s * MLA_HEADS, QK_ROPE)
    o_lat = _decode_attention(q_lat, q_rope, cache_ckv, jnp.swapaxes(cache_krope, 2, 3), page_table,
                              ckv_s.reshape(b_s, l_s, KV_LORA), kr_s.reshape(b_s, l_s, QK_ROPE), TILES["decode_pages"])
    mo_s = _head_matmul(o_lat.reshape(t_s, MLA_HEADS * KV_LORA), w_uv, 1, BF16)

    outs = []
    for x, big, ro, mo, p_emb in ((xp, big_p, ro_p, mo_p, p_prompt[0]), (xs, big_s, ro_s, mo_s, p_sample[0])):
        t = x.shape[0]
        merged = _merge(ro, mo, w_ret_o_b, w_mla_o_b, big, TILES["merge_rows"], TILES["merge_cols"])
        h, xn, xnt = _oproj(x, merged, w_o_b, norm2_g[0], TILES["out_proj_rows"])
        d, ea, b, eb_tab = _peer_scores(xn, peer_wq_b, k0, k1, TILES["peer_score_tokens"])
        y = _peer_dense(xnt, u_b, vt_b, d, ea, b, eb_tab, TILES["peer_tokens"], TILES["peer_experts"])
        outs.append(_ple(h, y, p_emb.reshape(t, PLE_DIM), norm3_g[0], w_gate_b, w_ple_b, final_norm_g,
                         TILES["ple_rows"]))

    y_prompt = outs[0].reshape(b_p, l_p, D_MODEL)
    y_sample = outs[1].reshape(b_s, l_s, D_MODEL)
    return (y_prompt, y_sample,
            ckv_p.reshape(1, b_p, l_p, KV_LORA), kr_p.reshape(1, b_p, l_p, QK_ROPE), state_p[None],
            ckv_s.reshape(1, b_s, l_s, KV_LORA), kr_s.reshape(1, b_s, l_s, QK_ROPE), state_s[None])
```
